```python
import math
import jax, jax.numpy as jnp
from jax import lax
import numpy as np

D_MODEL = 2048
BATCH = 16
SEQ = 256
DEPTH = 4
DEC_BATCH = 8
DEC_SEQ = 2048
PAST_LEN = 512

GRID_W = 64
N_MIXERS = 3
DA_HEADS = 8
DA_HEAD_DIM = 128
DA_V_DIM = 2 * DA_HEAD_DIM
DA_QK_WIDTH = 2 * DA_HEADS * DA_HEAD_DIM
DA_WIDTH = DA_HEADS * DA_V_DIM
RET_HEADS = 8
RET_QK_DIM = D_MODEL // RET_HEADS
RET_V_DIM = 2 * RET_QK_DIM
RET_QK_WIDTH = RET_HEADS * RET_QK_DIM
RET_WIDTH = RET_HEADS * RET_V_DIM
ML_HEADS = 4
ML_WIDTH = 2 * D_MODEL
ML_HEAD_DIM = ML_WIDTH // ML_HEADS
ML_BLOCK = 4
ML_CONV_W = 5
CHUNK = 128
Q_BLOCK = 128
ROPE_BASE = 10000.0
NORM_EPS = 1e-6

kernel_name = 'hybrid_diffattn_retention_mlstm_prefix_dit_step'


def rmsnorm(x, g):
    xf = x.astype(jnp.float32)
    y = xf * lax.rsqrt(jnp.mean(xf * xf, axis=-1, keepdims=True) + NORM_EPS)
    return (y * g.astype(jnp.float32)).astype(x.dtype)


def adaln(cond, w, b):
    m = jax.nn.silu(cond) @ w + b
    return jnp.split(m, 3, axis=-1)


def grid_rope(x):
    L = x.shape[1]
    rows = L // GRID_W
    t = jnp.arange(rows * GRID_W)
    row = (t // GRID_W).astype(jnp.float32)
    col = (t % GRID_W).astype(jnp.float32)
    half = x.shape[-1] // 2
    n_freq = half // 2
    inv = ROPE_BASE ** (-jnp.arange(n_freq, dtype=jnp.float32) / n_freq)

    def rot(xa, pos):
        ang = pos[:, None] * inv[None, :]
        cos = jnp.cos(ang)[None, :, None, :].astype(x.dtype)
        sin = jnp.sin(ang)[None, :, None, :].astype(x.dtype)
        x1, x2 = jnp.split(xa, 2, axis=-1)
        return jnp.concatenate([x1 * cos - x2 * sin, x2 * cos + x1 * sin], axis=-1)

    return jnp.concatenate([rot(x[..., :half], row), rot(x[..., half:], col)], axis=-1)


def to_chunks(a):
    B, L = a.shape[:2]
    return a.reshape((B, L // CHUNK, CHUNK) + a.shape[2:]).swapaxes(0, 1)


def from_chunks(a):
    nc, B = a.shape[:2]
    return a.swapaxes(0, 1).reshape((B, nc * CHUNK) + a.shape[3:])


def centred_dwconv(x, w, b):
    pad = (w.shape[0] - 1) // 2
    y = lax.conv_general_dilated(x, w[:, None, :].astype(x.dtype), window_strides=(1,),
                                 padding=[(pad, pad)], dimension_numbers=('NWC', 'WIO', 'NWC'),
                                 feature_group_count=x.shape[-1])
    return y + b


def diff_softmax_attention(q, k, v, lam):
    B, Lq, S, d = q.shape
    H = S // 2
    nb = Lq // Q_BLOCK
    scale = d ** -0.5
    qb = q.reshape(B, nb, Q_BLOCK, S, d).swapaxes(0, 1)

    def block(qi):
        s = jnp.einsum('bqsd,bksd->bsqk', qi, k).astype(jnp.float32) * scale
        pr = jax.nn.softmax(s, axis=-1).reshape(B, H, 2, Q_BLOCK, -1)
        a = pr[:, :, 0] - lam * pr[:, :, 1]
        return jnp.einsum('bhqk,bkhe->bqhe', a.astype(v.dtype), v)

    o = lax.map(block, qb)
    return o.swapaxes(0, 1).reshape(B, Lq, H, -1)


def diff_attention_mixer(h, p, layer_idx, ctx):
    B, L, _ = h.shape
    lam_init = 0.8 - 0.6 * math.exp(-0.3 * layer_idx)
    u = h @ p['in_w']
    q, k, v, g = jnp.split(u, [DA_QK_WIDTH, 2 * DA_QK_WIDTH, 2 * DA_QK_WIDTH + DA_WIDTH], axis=-1)
    q = rmsnorm(q.reshape(B, L, 2 * DA_HEADS, DA_HEAD_DIM), p['qk_g'][0])
    k = rmsnorm(k.reshape(B, L, 2 * DA_HEADS, DA_HEAD_DIM), p['qk_g'][1])
    v = v.reshape(B, L, DA_HEADS, DA_V_DIM)
    if ctx is None:
        keys, vals = k, v
        new = (k, v)
    else:
        q = grid_rope(q)
        keys = jnp.concatenate([grid_rope(k), ctx[0]], axis=1)
        vals = jnp.concatenate([v, ctx[1]], axis=1)
        new = None
    lm = p['lam'].astype(jnp.float32)
    lam = jnp.exp(jnp.sum(lm[0] * lm[1])) - jnp.exp(jnp.sum(lm[2] * lm[3])) + lam_init
    o = diff_softmax_attention(q, keys, vals, lam)
    o = rmsnorm(o, p['subln_g']) * (1.0 - lam_init)
    y = (o.reshape(B, L, DA_WIDTH) * jax.nn.silu(g)) @ p['out_w']
    return y, new


def retention_chunked(q, k, v, log_g, s0):
    pos = jnp.arange(CHUNK, dtype=jnp.float32)
    rel = pos[:, None] - pos[None, :]
    dmat = jnp.where(rel >= 0, jnp.exp(jnp.maximum(rel, 0.0)[None] * log_g[:, None, None]), 0.0)
    q_dec = jnp.exp((pos[:, None] + 1.0) * log_g[None, :])
    k_dec = jnp.exp((CHUNK - 1.0 - pos)[:, None] * log_g[None, :])
    c_dec = jnp.exp(CHUNK * log_g)

    def step(s, inp):
        qc, kc, vc = inp
        att = jnp.einsum('bihd,bjhd->bhij', qc, kc) * dmat[None]
        o = (jnp.einsum('bhij,bjhe->bihe', att, vc)
             + jnp.einsum('bihd,bhde->bihe', qc * q_dec[None, :, :, None], s))
        s = c_dec[None, :, None, None] * s + jnp.einsum('bjhd,bjhe->bhde', kc * k_dec[None, :, :, None], vc)
        return s, o

    s, o = lax.scan(step, s0, (to_chunks(q), to_chunks(k), to_chunks(v)))
    return from_chunks(o), s


def retention_mixer(h, p, layer_idx, ctx):
    B, L, _ = h.shape
    u = h @ p['in_w']
    q, k, v, g = jnp.split(u, [RET_QK_WIDTH, 2 * RET_QK_WIDTH, 2 * RET_QK_WIDTH + RET_WIDTH], axis=-1)
    q = q.reshape(B, L, RET_HEADS, RET_QK_DIM).astype(jnp.float32)
    k = k.reshape(B, L, RET_HEADS, RET_QK_DIM).astype(jnp.float32) * (RET_QK_DIM ** -0.5)
    v = v.reshape(B, L, RET_HEADS, RET_V_DIM).astype(jnp.float32)
    log_g = jnp.log1p(-jnp.exp(p['decay'].astype(jnp.float32)))
    if ctx is None:
        s0 = jnp.zeros((B, 2, RET_HEADS, RET_QK_DIM, RET_V_DIM), jnp.float32)
    else:
        s0 = ctx[0].astype(jnp.float32)
    of, sf = retention_chunked(q, k, v, log_g[0], s0[:, 0])
    ob, sb = retention_chunked(jnp.flip(q, 1), jnp.flip(k, 1), jnp.flip(v, 1), log_g[1], s0[:, 1])
    o = of + jnp.flip(ob, 1)
    o = rmsnorm(o, p['gn_g']).astype(h.dtype)
    y = (o.reshape(B, L, RET_WIDTH) * jax.nn.silu(g)) @ p['out_w']
    new = (jnp.stack([sf, sb], axis=1),) if ctx is None else None
    return y, new


def mlstm_chunked(q, k, v, li, lf, c0, n0, m0):
    causal = jnp.tril(jnp.ones((CHUNK, CHUNK), dtype=bool))

    def step(carry, inp):
        cm, n, m = carry
        qc, kc, vc, ic, fc = inp
        bh = jnp.cumsum(fc, axis=1).transpose(0, 2, 1)
        ih = ic.transpose(0, 2, 1)
        tot = bh[:, :, -1]
        dlog = bh[:, :, :, None] - bh[:, :, None, :] + ih[:, :, None, :]
        dlog = jnp.where(causal[None, None], dlog, -jnp.inf)
        inter = bh + m[:, :, None]
        m_i = jnp.maximum(dlog.max(-1), inter)
        w = jnp.exp(dlog - m_i[..., None])
        wi = jnp.exp(inter - m_i)
        s = jnp.einsum('bihd,bjhd->bhij', qc, kc) * w
        num = (jnp.einsum('bhij,bjhe->bihe', s, vc)
               + wi.transpose(0, 2, 1)[..., None] * jnp.einsum('bihd,bhde->bihe', qc, cm))
        den = s.sum(-1) + wi * jnp.einsum('bihd,bhd->bhi', qc, n)
        norm = jnp.maximum(jnp.abs(den), jnp.exp(-m_i)).transpose(0, 2, 1)[..., None]
        hc = num / norm
        kw = tot[:, :, None] - bh + ih
        m_new = jnp.maximum(tot + m, kw.max(-1))
        wk = jnp.exp(kw - m_new[..., None])
        dec = jnp.exp(tot + m - m_new)
        kcw = kc * wk.transpose(0, 2, 1)[..., None]
        cm = dec[..., None, None] * cm + jnp.einsum('bjhd,bjhe->bhde', kcw, vc)
        n = dec[..., None] * n + kcw.sum(axis=1)
        return (cm, n, m_new), hc

    (cm, n, m), h = lax.scan(step, (c0, n0, m0),
                             (to_chunks(q), to_chunks(k), to_chunks(v), to_chunks(li), to_chunks(lf)))
    return from_chunks(h), cm, n, m


def mlstm_mixer(h, p, layer_idx, ctx):
    B, L, _ = h.shape
    u = h @ p['in_w']
    xm, z = jnp.split(u, 2, axis=-1)
    xc = jax.nn.silu(centred_dwconv(xm, p['conv_w'], p['conv_b']))

    def headwise(a, w):
        return jnp.einsum('blgi,gio->blgo', a.reshape(B, L, -1, ML_BLOCK), w).reshape(B, L, ML_WIDTH)

    q = headwise(xc, p['qkv_bd'][0])
    k = headwise(xc, p['qkv_bd'][1])
    v = headwise(xm, p['qkv_bd'][2])
    gates = (jnp.einsum('blc,rcg->rblg', jnp.concatenate([q, k, v], axis=-1), p['gate_w']).astype(jnp.float32)
             + p['gate_b'].astype(jnp.float32)[:, None, None, :])
    li = gates[..., :ML_HEADS]
    lf = jax.nn.log_sigmoid(gates[..., ML_HEADS:])
    qh = q.reshape(B, L, ML_HEADS, ML_HEAD_DIM).astype(jnp.float32)
    kh = k.reshape(B, L, ML_HEADS, ML_HEAD_DIM).astype(jnp.float32) * (ML_HEAD_DIM ** -0.5)
    vh = v.reshape(B, L, ML_HEADS, ML_HEAD_DIM).astype(jnp.float32)
    if ctx is None:
        c0 = jnp.zeros((B, 2, ML_HEADS, ML_HEAD_DIM, ML_HEAD_DIM), jnp.float32)
        n0 = jnp.zeros((B, 2, ML_HEADS, ML_HEAD_DIM), jnp.float32)
        m0 = jnp.zeros((B, 2, ML_HEADS), jnp.float32)
    else:
        c0, n0, m0 = (a.astype(jnp.float32) for a in ctx)
    hf, cf, nf, mf = mlstm_chunked(qh, kh, vh, li[0], lf[0], c0[:, 0], n0[:, 0], m0[:, 0])
    hb, cb, nb, mb = mlstm_chunked(jnp.flip(qh, 1), jnp.flip(kh, 1), jnp.flip(vh, 1),
                                   jnp.flip(li[1], 1), jnp.flip(lf[1], 1), c0[:, 1], n0[:, 1], m0[:, 1])
    o = hf + jnp.flip(hb, 1)
    o = rmsnorm(o, p['gn_g']).reshape(B, L, ML_WIDTH).astype(h.dtype)
    o = (o + p['skip'] * xc) * jax.nn.silu(z)
    y = o @ p['out_w']
    if ctx is None:
        new = (jnp.stack([cf, cb], axis=1), jnp.stack([nf, nb], axis=1), jnp.stack([mf, mb], axis=1))
    else:
        new = None
    return y, new


def setup_inputs(seed: int = 0) -> dict:
    key = jax.random.key(seed)
    ks = iter(jax.random.split(key, 128))
    f32 = jnp.float32
    d = D_MODEL

    def nrm(shape, scale=1.0):
        return scale * jax.random.normal(next(ks), shape, f32)

    def gain(shape):
        return 1.0 + 0.1 * nrm(shape)

    inp = {}
    inp['x_prompt'] = nrm((BATCH, SEQ, d))
    inp['x_sample'] = nrm((DEC_BATCH, DEC_SEQ, d))
    inp['c'] = nrm((DEC_BATCH, d))
    inp['c_ctx'] = nrm((d,))
    inp['cache_k_l0'] = nrm((DEC_BATCH, PAST_LEN, 2 * DA_HEADS, DA_HEAD_DIM))
    inp['cache_v_l0'] = nrm((DEC_BATCH, PAST_LEN, DA_HEADS, DA_V_DIM))
    inp['state_ret_l1'] = nrm((DEC_BATCH, 2, RET_HEADS, RET_QK_DIM, RET_V_DIM), 0.1)
    inp['state_C_l2'] = nrm((DEC_BATCH, 2, ML_HEADS, ML_HEAD_DIM, ML_HEAD_DIM), 0.1)
    inp['state_n_l2'] = nrm((DEC_BATCH, 2, ML_HEADS, ML_HEAD_DIM), 0.1)
    inp['state_m_l2'] = nrm((DEC_BATCH, 2, ML_HEADS))
    inp['cache_k_l3'] = nrm((DEC_BATCH, PAST_LEN, 2 * DA_HEADS, DA_HEAD_DIM))
    inp['cache_v_l3'] = nrm((DEC_BATCH, PAST_LEN, DA_HEADS, DA_V_DIM))
    for i in range(DEPTH):
        kind = i % N_MIXERS
        inp[f'norm_g_l{i}'] = gain((d,))
        inp[f'ada_w_l{i}'] = nrm((d, 3 * d), d ** -0.5)
        inp[f'ada_b_l{i}'] = nrm((3 * d,), 0.02)
        if kind == 0:
            inp[f'in_w_l{i}'] = nrm((d, 2 * DA_QK_WIDTH + 2 * DA_WIDTH), d ** -0.5)
            inp[f'out_w_l{i}'] = nrm((DA_WIDTH, d), DA_WIDTH ** -0.5)
            inp[f'qk_g_l{i}'] = gain((2, DA_HEAD_DIM))
            inp[f'lam_l{i}'] = nrm((4, DA_HEAD_DIM), 0.1)
            inp[f'subln_g_l{i}'] = gain((DA_V_DIM,))
        elif kind == 1:
            inp[f'in_w_l{i}'] = nrm((d, 2 * RET_QK_WIDTH + 2 * RET_WIDTH), d ** -0.5)
            inp[f'out_w_l{i}'] = nrm((RET_WIDTH, d), RET_WIDTH ** -0.5)
            inp[f'decay_l{i}'] = (-(5.0 + jnp.arange(RET_HEADS, dtype=f32))[None, :] * math.log(2.0)
                                  + nrm((2, RET_HEADS), 0.05))
            inp[f'gn_g_l{i}'] = gain((RET_HEADS, RET_V_DIM))
        else:
            inp[f'in_w_l{i}'] = nrm((d, 2 * ML_WIDTH), d ** -0.5)
            inp[f'out_w_l{i}'] = nrm((ML_WIDTH, d), ML_WIDTH ** -0.5)
            inp[f'conv_w_l{i}'] = nrm((ML_CONV_W, ML_WIDTH), ML_CONV_W ** -0.5)
            inp[f'conv_b_l{i}'] = nrm((ML_WIDTH,), 0.02)
            inp[f'qkv_bd_l{i}'] = nrm((3, ML_WIDTH // ML_BLOCK, ML_BLOCK, ML_BLOCK), ML_BLOCK ** -0.5)
            inp[f'gate_w_l{i}'] = nrm((2, 3 * ML_WIDTH, 2 * ML_HEADS), (3 * ML_WIDTH) ** -0.5)
            inp[f'gate_b_l{i}'] = jnp.concatenate(
                [nrm((2, ML_HEADS), 0.1), jax.random.uniform(next(ks), (2, ML_HEADS), f32, 3.0, 6.0)], axis=-1)
            inp[f'gn_g_l{i}'] = gain((ML_HEADS, ML_HEAD_DIM))
            inp[f'skip_l{i}'] = gain((ML_WIDTH,))
    return inp


def reference(x_prompt, x_sample, c, c_ctx,
              cache_k_l0, cache_v_l0, state_ret_l1, state_C_l2, state_n_l2, state_m_l2,
              cache_k_l3, cache_v_l3,
              norm_g_l0, ada_w_l0, ada_b_l0, in_w_l0, out_w_l0, qk_g_l0, lam_l0, subln_g_l0,
              norm_g_l1, ada_w_l1, ada_b_l1, in_w_l1, out_w_l1, decay_l1, gn_g_l1,
              norm_g_l2, ada_w_l2, ada_b_l2, in_w_l2, out_w_l2, conv_w_l2, conv_b_l2, qkv_bd_l2,
              gate_w_l2, gate_b_l2, gn_g_l2, skip_l2,
              norm_g_l3, ada_w_l3, ada_b_l3, in_w_l3, out_w_l3, qk_g_l3, lam_l3, subln_g_l3):
    mixers = (diff_attention_mixer, retention_mixer, mlstm_mixer)
    layers = [
        dict(norm_g=norm_g_l0, ada_w=ada_w_l0, ada_b=ada_b_l0, in_w=in_w_l0, out_w=out_w_l0,
             qk_g=qk_g_l0, lam=lam_l0, subln_g=subln_g_l0),
        dict(norm_g=norm_g_l1, ada_w=ada_w_l1, ada_b=ada_b_l1, in_w=in_w_l1, out_w=out_w_l1,
             decay=decay_l1, gn_g=gn_g_l1),
        dict(norm_g=norm_g_l2, ada_w=ada_w_l2, ada_b=ada_b_l2, in_w=in_w_l2, out_w=out_w_l2,
             conv_w=conv_w_l2, conv_b=conv_b_l2, qkv_bd=qkv_bd_l2, gate_w=gate_w_l2, gate_b=gate_b_l2,
             gn_g=gn_g_l2, skip=skip_l2),
        dict(norm_g=norm_g_l3, ada_w=ada_w_l3, ada_b=ada_b_l3, in_w=in_w_l3, out_w=out_w_l3,
             qk_g=qk_g_l3, lam=lam_l3, subln_g=subln_g_l3),
    ]
    ctx_caches = [(cache_k_l0, cache_v_l0), (state_ret_l1,), (state_C_l2, state_n_l2, state_m_l2),
                  (cache_k_l3, cache_v_l3)]
    xp, xs = x_prompt, x_sample
    new_states = []
    for i in range(DEPTH):
        p = layers[i]
        mixer = mixers[i % N_MIXERS]
        sh_p, sc_p, gt_p = adaln(c_ctx, p['ada_w'], p['ada_b'])
        sh_s, sc_s, gt_s = adaln(c[:, None, :], p['ada_w'], p['ada_b'])
        hp = rmsnorm(xp, p['norm_g']) * (1.0 + sc_p) + sh_p
        yp, st = mixer(hp, p, i, None)
        hs = rmsnorm(xs, p['norm_g']) * (1.0 + sc_s) + sh_s
        ys, _ = mixer(hs, p, i, ctx_caches[i])
        xp = xp + gt_p * yp
        xs = xs + gt_s * ys
        new_states.append(st)
    (k_l0, v_l0), (ret_l1,), (C_l2, n_l2, m_l2), (k_l3, v_l3) = new_states
    return (xp, xs, k_l0, v_l0, ret_l1, C_l2, n_l2, m_l2, k_l3, v_l3)
```

```python
import functools
import math

import jax
import jax.numpy as jnp
from jax import lax
from jax.experimental import pallas as pl
from jax.experimental.pallas import tpu as pltpu

F32 = jnp.float32
BF16 = jnp.bfloat16

CHUNK = 128
GRID_W = 64
ROPE_BASE = 10000.0
NORM_EPS = 1e-6
LANES = 128
VMEM_LIMIT_BYTES = 56 * 1024 * 1024

NT_DIMS = (((1,), (1,)), ((), ()))
TN_DIMS = (((0,), (0,)), ((), ()))


def _tile(n, pref):
    if n <= pref:
        return n
    t = pref
    while n % t:
        t -= 1
    return t


def _params(*sem):
    return pltpu.CompilerParams(dimension_semantics=sem, vmem_limit_bytes=VMEM_LIMIT_BYTES)


def _silu(x):
    return x * jax.nn.sigmoid(x)


def _rms(x):
    return x * lax.rsqrt(jnp.mean(x * x, axis=-1, keepdims=True) + NORM_EPS)


def _adaln_kernel(c_ref, w_ref, b_ref, o_ref):
    s = _silu(c_ref[...]).astype(BF16)
    o_ref[...] = jnp.dot(s, w_ref[...].astype(BF16), preferred_element_type=F32) + b_ref[...]


def _adaln(cond, w, b):
    R, D = cond.shape
    N = w.shape[1]
    tn = _tile(N, 512)
    return pl.pallas_call(
        _adaln_kernel,
        grid=(N // tn,),
        in_specs=[pl.BlockSpec((R, D), lambda j: (0, 0)),
                  pl.BlockSpec((D, tn), lambda j: (0, j)),
                  pl.BlockSpec((1, tn), lambda j: (0, j))],
        out_specs=pl.BlockSpec((R, tn), lambda j: (0, j)),
        out_shape=jax.ShapeDtypeStruct((R, N), F32),
        compiler_params=_params("arbitrary"),
        name="adaln",
    )(cond, w, b.reshape(1, N))


def _inproj_kernel(x_ref, g_ref, sc_ref, sh_ref, w_ref, o_ref, h_ref):
    @pl.when(pl.program_id(1) == 0)
    def _():
        y = _rms(x_ref[...]) * g_ref[...]
        h_ref[...] = (y * (1.0 + sc_ref[0]) + sh_ref[0]).astype(BF16)

    o_ref[...] = jnp.dot(h_ref[...], w_ref[...], preferred_element_type=F32).astype(o_ref.dtype)


def _inproj(x2, norm_g, scale, shift, w, rows_per_mod, out_dtype):
    rows, D = x2.shape
    N = w.shape[1]
    tm = _tile(rows_per_mod, 1024)
    tn = _tile(N, 512)
    mod = lambda i, j: ((i * tm) // rows_per_mod, 0, 0)
    return pl.pallas_call(
        _inproj_kernel,
        grid=(rows // tm, N // tn),
        in_specs=[pl.BlockSpec((tm, D), lambda i, j: (i, 0)),
                  pl.BlockSpec((1, D), lambda i, j: (0, 0)),
                  pl.BlockSpec((1, 1, D), mod),
                  pl.BlockSpec((1, 1, D), mod),
                  pl.BlockSpec((D, tn), lambda i, j: (0, j))],
        out_specs=pl.BlockSpec((tm, tn), lambda i, j: (i, j)),
        out_shape=jax.ShapeDtypeStruct((rows, N), out_dtype),
        scratch_shapes=[pltpu.VMEM((tm, D), BF16)],
        compiler_params=_params("parallel", "arbitrary"),
        name="inproj",
    )(x2, norm_g.reshape(1, D), scale, shift, w)


def _outproj_kernel(a_ref, w_ref, x_ref, gt_ref, o_ref):
    y = jnp.dot(a_ref[...], w_ref[...], preferred_element_type=F32)
    o_ref[...] = x_ref[...] + gt_ref[0] * y


def _outproj(a, w, x2, gate, rows_per_mod):
    rows, K = a.shape
    D = w.shape[1]
    tm = _tile(rows_per_mod, 1024)
    tn = _tile(D, 512)
    return pl.pallas_call(
        _outproj_kernel,
        grid=(rows // tm, D // tn),
        in_specs=[pl.BlockSpec((tm, K), lambda i, j: (i, 0)),
                  pl.BlockSpec((K, tn), lambda i, j: (0, j)),
                  pl.BlockSpec((tm, tn), lambda i, j: (i, j)),
                  pl.BlockSpec((1, 1, tn), lambda i, j: ((i * tm) // rows_per_mod, 0, j))],
        out_specs=pl.BlockSpec((tm, tn), lambda i, j: (i, j)),
        out_shape=jax.ShapeDtypeStruct((rows, D), F32),
        compiler_params=_params("parallel", "arbitrary"),
        name="outproj",
    )(a, w, x2, gate)


def _rope_tables(L, dh):
    t = jnp.arange(L)
    row = (t // GRID_W).astype(F32)
    col = (t % GRID_W).astype(F32)
    n_freq = dh // 4
    inv = ROPE_BASE ** (-jnp.arange(n_freq, dtype=F32) / n_freq)
    ar = row[:, None] * inv[None, :]
    ac = col[:, None] * inv[None, :]
    cos = jnp.concatenate([jnp.cos(ar), jnp.cos(ar), jnp.cos(ac), jnp.cos(ac)], axis=-1)
    sin = jnp.concatenate([-jnp.sin(ar), jnp.sin(ar), -jnp.sin(ac), jnp.sin(ac)], axis=-1)
    return cos.astype(F32), sin.astype(F32)


def _qkpost_kernel(*refs, n_sub, dh, rope, emit_k):
    if rope:
        u_ref, g_ref, cos_ref, sin_ref = refs[:4]
        outs = refs[4:]
    else:
        u_ref, g_ref = refs[:2]
        outs = refs[2:]
    o_ref = outs[0]
    g = g_ref[0]
    if rope:
        cos = cos_ref[...]
        sin = sin_ref[...]
        lane = lax.broadcasted_iota(jnp.int32, cos.shape, 1)
        first = (lane % (dh // 2)) < (dh // 4)
    for s in range(n_sub):
        sl = slice(s * dh, (s + 1) * dh)
        y = _rms(u_ref[:, sl].astype(F32)) * g
        if rope:
            swapped = jnp.where(first, pltpu.roll(y, dh - dh // 4, 1), pltpu.roll(y, dh // 4, 1))
            y = y * cos + swapped * sin
        o_ref[:, sl] = y.astype(BF16)
        if emit_k:
            outs[1][:, sl] = y


def _qkpost(u, qk_g, L, n_sub, dh, rope, emit_k):
    rows = u.shape[0]
    W = n_sub * dh
    tm = _tile(L, 512)
    nl = L // tm
    in_specs = [pl.BlockSpec((tm, W), lambda i, j: (i, j)),
                pl.BlockSpec((1, 1, dh), lambda i, j: (j, 0, 0))]
    args = [u, qk_g.reshape(2, 1, dh)]
    if rope:
        cos, sin = _rope_tables(L, dh)
        in_specs += [pl.BlockSpec((tm, dh), lambda i, j: (i % nl, 0))] * 2
        args += [cos, sin]
    out_specs = [pl.BlockSpec((tm, W), lambda i, j: (i, j))]
    out_shape = [jax.ShapeDtypeStruct((rows, 2 * W), BF16)]
    if emit_k:
        out_specs.append(pl.BlockSpec((tm, W), lambda i, j: (i, 0)))
        out_shape.append(jax.ShapeDtypeStruct((rows, W), F32))
    return pl.pallas_call(
        functools.partial(_qkpost_kernel, n_sub=n_sub, dh=dh, rope=rope, emit_k=emit_k),
        grid=(rows // tm, 2),
        in_specs=in_specs,
        out_specs=out_specs,
        out_shape=out_shape,
        compiler_params=_params("parallel", "arbitrary"),
        name="qkpost",
    )(*args)


def _attn_kernel(*refs, dh, lam_init, has_ctx):
    if has_ctx:
        lam_ref, sg_ref, q_ref, k_ref, v_ref, g_ref, ck_ref, cv_ref, o_ref = refs
    else:
        lam_ref, sg_ref, q_ref, k_ref, v_ref, g_ref, o_ref = refs
    lm = lam_ref[...]
    lam = (jnp.exp(jnp.sum(lm[0:1] * lm[1:2], axis=1, keepdims=True))
           - jnp.exp(jnp.sum(lm[2:3] * lm[3:4], axis=1, keepdims=True)) + lam_init)
    scale = dh ** -0.5
    a_self = None
    a_ctx = None
    for c in range(2):
        sl = slice(c * dh, (c + 1) * dh)
        qc = q_ref[:, sl]
        s1 = lax.dot_general(qc, k_ref[:, sl], NT_DIMS, preferred_element_type=F32) * scale
        m = jnp.max(s1, axis=-1, keepdims=True)
        if has_ctx:
            s2 = lax.dot_general(qc, ck_ref[0, :, sl], NT_DIMS, preferred_element_type=F32) * scale
            m = jnp.maximum(m, jnp.max(s2, axis=-1, keepdims=True))
        p1 = jnp.exp(s1 - m)
        l = jnp.sum(p1, axis=-1, keepdims=True)
        if has_ctx:
            p2 = jnp.exp(s2 - m)
            l = l + jnp.sum(p2, axis=-1, keepdims=True)
        coef = (1.0 / l) if c == 0 else (-lam / l)
        a_self = p1 * coef if a_self is None else a_self + p1 * coef
        if has_ctx:
            a_ctx = p2 * coef if a_ctx is None else a_ctx + p2 * coef
    o = jnp.dot(a_self.astype(BF16), v_ref[...].astype(BF16), preferred_element_type=F32)
    if has_ctx:
        o = o + jnp.dot(a_ctx.astype(BF16), cv_ref[0], preferred_element_type=F32)
    o = _rms(o) * sg_ref[...] * (1.0 - lam_init)
    o_ref[...] = (o * _silu(g_ref[...].astype(F32))).astype(BF16)


def _attention(qk, u, lam_p, subln_g, ctx, B, L, H, dh, lam_init):
    rows = qk.shape[0]
    V = 2 * dh
    tq = _tile(L, 256)
    nq = L // tq
    has_ctx = ctx is not None
    in_specs = [pl.BlockSpec((4, dh), lambda b, h, i: (0, 0)),
                pl.BlockSpec((1, V), lambda b, h, i: (0, 0)),
                pl.BlockSpec((tq, V), lambda b, h, i: (b * nq + i, h)),
                pl.BlockSpec((L, V), lambda b, h, i: (b, H + h)),
                pl.BlockSpec((L, V), lambda b, h, i: (b, 2 * H + h)),
                pl.BlockSpec((tq, V), lambda b, h, i: (b * nq + i, 3 * H + h))]
    args = [lam_p, subln_g.reshape(1, V), qk, qk, u, u]
    if has_ctx:
        P = ctx[0].shape[1]
        in_specs += [pl.BlockSpec((1, P, V), lambda b, h, i: (b, 0, h))] * 2
        args += list(ctx)
    return pl.pallas_call(
        functools.partial(_attn_kernel, dh=dh, lam_init=lam_init, has_ctx=has_ctx),
        grid=(B, H, nq),
        in_specs=in_specs,
        out_specs=pl.BlockSpec((tq, V), lambda b, h, i: (b * nq + i, h)),
        out_shape=jax.ShapeDtypeStruct((rows, H * V), BF16),
        compiler_params=_params("parallel", "parallel", "arbitrary"),
        name="diffattn",
    )(*args)


def _diff_attention_layer(x2, B, L, p, mods, rows_per_mod, layer_idx, ctx):
    shift, scale, gate = mods
    dh = p['qk_g'].shape[1]
    V = p['subln_g'].shape[0]
    QKW = (p['in_w'].shape[1] - 2 * p['out_w'].shape[0]) // 2
    H = p['out_w'].shape[0] // V
    n_sub = QKW // dh
    lam_init = 0.8 - 0.6 * math.exp(-0.3 * layer_idx)
    is_prompt = ctx is None
    u = _inproj(x2, p['norm_g'], scale, shift, p['in_w'].astype(BF16), rows_per_mod,
                F32 if is_prompt else BF16)
    res = _qkpost(u, p['qk_g'], L, n_sub, dh, rope=not is_prompt, emit_k=is_prompt)
    new = None
    if is_prompt:
        qk, kf = res
        new = (kf.reshape(B, L, n_sub, dh), u[:, 2 * QKW:2 * QKW + H * V].reshape(B, L, H, V))
        cache = None
    else:
        qk = res[0]
        ck, cv = ctx
        P = ck.shape[1]
        cache = (ck.reshape(B, P, QKW).astype(BF16), cv.reshape(B, P, H * V).astype(BF16))
    o = _attention(qk, u, p['lam'], p['subln_g'], cache, B, L, H, dh, lam_init)
    y = _outproj(o, p['out_w'].astype(BF16), x2, gate, rows_per_mod)
    return y, new


def _ret_kernel(*refs, nc, dk, has_s0, emit_state):
    dec_ref, gn_ref, q_ref, k_ref, v_ref, g_ref = refs[:6]
    i = 6
    if has_s0:
        s0_ref = refs[i]
        i += 1
    o_ref = refs[i]
    i += 1
    if emit_state:
        so_ref = refs[i]
        i += 1
    sf_ref, sb_ref, acc_ref = refs[i:]
    C = CHUNK
    lg = jnp.log1p(-jnp.exp(dec_ref[0]))
    lgf = lg[0:1]
    lgb = lg[1:2]
    ii = lax.broadcasted_iota(jnp.int32, (C, C), 0)
    jj = lax.broadcasted_iota(jnp.int32, (C, C), 1)
    rel = (ii - jj).astype(F32)
    dtot = (jnp.where(rel >= 0, jnp.exp(jnp.maximum(rel, 0.0) * lgf), 0.0)
            + jnp.where(rel <= 0, jnp.exp(jnp.maximum(-rel, 0.0) * lgb), 0.0))
    pos = lax.broadcasted_iota(jnp.int32, (C, 1), 0).astype(F32)
    qdf = jnp.exp((pos + 1.0) * lgf)
    kdf = jnp.exp((C - 1.0 - pos) * lgf)
    qdb = jnp.exp((C - pos) * lgb)
    kdb = jnp.exp(pos * lgb)
    cdf = jnp.exp(C * lgf)
    cdb = jnp.exp(C * lgb)
    scale = dk ** -0.5
    if has_s0:
        sf_ref[...] = s0_ref[0, 0, 0]
        sb_ref[...] = s0_ref[0, 1, 0]
    else:
        sf_ref[...] = jnp.zeros_like(sf_ref)
        sb_ref[...] = jnp.zeros_like(sb_ref)
    gn = gn_ref[0]

    def load(c):
        r = pl.multiple_of(c * C, C)
        q = q_ref[pl.ds(r, C), :]
        k = k_ref[pl.ds(r, C), :].astype(F32) * scale
        v = v_ref[pl.ds(r, C), :].astype(BF16)
        return r, q, k, v

    def fwd(c, carry):
        r, q, k, v = load(c)
        att = lax.dot_general(q, k.astype(BF16), NT_DIMS, preferred_element_type=F32) * dtot
        o = jnp.dot(att.astype(BF16), v, preferred_element_type=F32)
        o = o + jnp.dot((q.astype(F32) * qdf).astype(BF16), sf_ref[...].astype(BF16),
                        preferred_element_type=F32)
        acc_ref[pl.ds(r, C), :] = o
        sf_ref[...] = cdf * sf_ref[...] + lax.dot_general(
            (k * kdf).astype(BF16), v, TN_DIMS, preferred_element_type=F32)
        return carry

    lax.fori_loop(0, nc, fwd, 0)

    def bwd(t, carry):
        r, q, k, v = load(nc - 1 - t)
        o = acc_ref[pl.ds(r, C), :] + jnp.dot(
            (q.astype(F32) * qdb).astype(BF16), sb_ref[...].astype(BF16), preferred_element_type=F32)
        sb_ref[...] = cdb * sb_ref[...] + lax.dot_general(
            (k * kdb).astype(BF16), v, TN_DIMS, preferred_element_type=F32)
        g = g_ref[pl.ds(r, C), :].astype(F32)
        o_ref[pl.ds(r, C), :] = (_rms(o) * gn * _silu(g)).astype(BF16)
        return carry

    lax.fori_loop(0, nc, bwd, 0)
    if emit_state:
        so_ref[0, 0, 0] = sf_ref[...]
        so_ref[0, 1, 0] = sb_ref[...]


def _retention_layer(x2, B, L, p, mods, rows_per_mod, ctx):
    shift, scale, gate = mods
    H, dv = p['gn_g'].shape
    dk = (p['in_w'].shape[1] - 2 * H * dv) // (2 * H)
    nc = L // CHUNK
    rows = x2.shape[0]
    has_s0 = ctx is not None
    emit_state = not has_s0
    u = _inproj(x2, p['norm_g'], scale, shift, p['in_w'].astype(BF16), rows_per_mod, BF16)
    kq = (H * dk) // dk
    kv = (2 * H * dk) // dv
    in_specs = [pl.BlockSpec((1, 2, 1), lambda b, h: (h, 0, 0)),
                pl.BlockSpec((1, 1, dv), lambda b, h: (h, 0, 0)),
                pl.BlockSpec((L, dk), lambda b, h: (b, h)),
                pl.BlockSpec((L, dk), lambda b, h: (b, kq + h)),
                pl.BlockSpec((L, dv), lambda b, h: (b, kv + h)),
                pl.BlockSpec((L, dv), lambda b, h: (b, kv + H + h))]
    args = [p['decay'].T.reshape(H, 2, 1), p['gn_g'].reshape(H, 1, dv), u, u, u, u]
    st_spec = pl.BlockSpec((1, 2, 1, dk, dv), lambda b, h: (b, 0, h, 0, 0))
    if has_s0:
        in_specs.append(st_spec)
        args.append(ctx[0])
    out_specs = [pl.BlockSpec((L, dv), lambda b, h: (b, h))]
    out_shape = [jax.ShapeDtypeStruct((rows, H * dv), BF16)]
    if emit_state:
        out_specs.append(st_spec)
        out_shape.append(jax.ShapeDtypeStruct((B, 2, H, dk, dv), F32))
    res = pl.pallas_call(
        functools.partial(_ret_kernel, nc=nc, dk=dk, has_s0=has_s0, emit_state=emit_state),
        grid=(B, H),
        in_specs=in_specs,
        out_specs=out_specs,
        out_shape=out_shape,
        scratch_shapes=[pltpu.VMEM((dk, dv), F32), pltpu.VMEM((dk, dv), F32), pltpu.VMEM((L, dv), F32)],
        compiler_params=_params("parallel", "parallel"),
        name="retention",
    )(*args)
    y = _outproj(res[0], p['out_w'].astype(BF16), x2, gate, rows_per_mod)
    return y, ((res[1],) if emit_state else None)


def _mlpre_kernel(xm_ref, cw_ref, cb_ref, wbd_ref, gw_ref, xc_ref, q_ref, k_ref, v_ref, gt_ref, *, bw):
    L, tc = xm_ref.shape
    x = xm_ref[...].astype(F32)
    cw = cw_ref[...]
    W = cw.shape[0]
    pad = (W - 1) // 2
    row = lax.broadcasted_iota(jnp.int32, (L, tc), 0)
    acc = x * cw[pad:pad + 1] + cb_ref[...]
    for w in range(W):
        d = w - pad
        if d == 0:
            continue
        sh = pltpu.roll(x, (-d) % L, 0)
        valid = (row < L - d) if d > 0 else (row >= -d)
        acc = acc + jnp.where(valid, sh, 0.0) * cw[w:w + 1]
    xc = _silu(acc)
    xc_ref[...] = xc.astype(BF16)
    xcb = xc.astype(BF16)
    xmb = xm_ref[...].astype(BF16)
    gsum = jnp.zeros(gt_ref.shape, F32)
    for blk in range(tc // bw):
        sl = slice(blk * bw, (blk + 1) * bw)
        q = jnp.dot(xcb[:, sl], wbd_ref[0, blk], preferred_element_type=F32).astype(BF16)
        k = jnp.dot(xcb[:, sl], wbd_ref[1, blk], preferred_element_type=F32).astype(BF16)
        v = jnp.dot(xmb[:, sl], wbd_ref[2, blk], preferred_element_type=F32).astype(BF16)
        q_ref[:, sl] = q
        k_ref[:, sl] = k
        v_ref[:, sl] = v
        gsum = gsum + jnp.dot(q, gw_ref[0, sl, :], preferred_element_type=F32)
        gsum = gsum + jnp.dot(k, gw_ref[1, sl, :], preferred_element_type=F32)
        gsum = gsum + jnp.dot(v, gw_ref[2, sl, :], preferred_element_type=F32)

    @pl.when(pl.program_id(1) == 0)
    def _():
        gt_ref[...] = gsum

    @pl.when(pl.program_id(1) != 0)
    def _():
        gt_ref[...] = gt_ref[...] + gsum


def _logsigmoid(x):
    return jnp.minimum(x, 0.0) - jnp.log1p(jnp.exp(-jnp.abs(x)))


def _mlscan_kernel(*refs, nc, dk, has_s0, emit_state):
    gl_ref, gb_ref, q_ref, k_ref, v_ref = refs[:5]
    i = 5
    if has_s0:
        c0_ref, n0_ref, m0_ref = refs[i:i + 3]
        i += 3
    h_ref = refs[i]
    i += 1
    if emit_state:
        co_ref, no_ref, mo_ref = refs[i:i + 3]
        i += 3
    c_ref, n_ref, m_ref = refs[i:]
    C = CHUNK
    d = pl.program_id(2)
    c = pl.program_id(3)

    @pl.when(c == 0)
    def _():
        if has_s0:
            c_ref[...] = c0_ref[0, 0, 0]
            n_ref[...] = n0_ref[0, 0, 0]
            m_ref[...] = m0_ref[0, 0, 0]
        else:
            c_ref[...] = jnp.zeros_like(c_ref)
            n_ref[...] = jnp.zeros_like(n_ref)
            m_ref[...] = jnp.zeros_like(m_ref)

    gl = gl_ref[0, 0, 0, 0] + gb_ref[0, 0]
    ih_row = gl[0:1]
    f_row = _logsigmoid(gl[1:2])
    ii = lax.broadcasted_iota(jnp.int32, (C, C), 0)
    jj = lax.broadcasted_iota(jnp.int32, (C, C), 1)
    sgn = 1 - 2 * d
    seen = (ii - jj) * sgn >= 0
    seen_t = (jj - ii) * sgn >= 0
    eye = ii == jj
    fmat = jnp.broadcast_to(f_row, (C, C))
    imat = jnp.broadcast_to(ih_row, (C, C))
    bh_col = jnp.sum(jnp.where(seen, fmat, 0.0), axis=1, keepdims=True)
    f_col = jnp.sum(jnp.where(eye, fmat, 0.0), axis=1, keepdims=True)
    ih_col = jnp.sum(jnp.where(eye, imat, 0.0), axis=1, keepdims=True)
    bh_row = jnp.sum(jnp.where(seen_t, jnp.broadcast_to(f_col, (C, C)), 0.0), axis=0, keepdims=True)
    tot = jnp.sum(f_row, axis=1, keepdims=True)
    m_prev = m_ref[...]
    dlog = jnp.where(seen, bh_col - bh_row + ih_row, -jnp.inf)
    inter = bh_col + m_prev
    m_i = jnp.maximum(jnp.max(dlog, axis=1, keepdims=True), inter)
    w = jnp.exp(dlog - m_i)
    wi = jnp.exp(inter - m_i)
    q = q_ref[...]
    kf = k_ref[...].astype(F32) * (dk ** -0.5)
    v = v_ref[...]
    s = lax.dot_general(q, kf.astype(BF16), NT_DIMS, preferred_element_type=F32) * w
    num = jnp.dot(s.astype(BF16), v, preferred_element_type=F32)
    num = num + wi * jnp.dot(q, c_ref[...].astype(BF16), preferred_element_type=F32)
    qn = jnp.sum(q.astype(F32) * n_ref[...], axis=1, keepdims=True)
    den = jnp.sum(s, axis=1, keepdims=True) + wi * qn
    norm = jnp.maximum(jnp.abs(den), jnp.exp(-m_i))
    h_ref[0] = num / norm
    kw = tot - bh_col + ih_col
    m_new = jnp.maximum(tot + m_prev, jnp.max(kw, axis=0, keepdims=True))
    wk = jnp.exp(kw - m_new)
    dec = jnp.exp(tot + m_prev - m_new)
    kcw = kf * wk
    c_ref[...] = dec * c_ref[...] + lax.dot_general(kcw.astype(BF16), v, TN_DIMS,
                                                   preferred_element_type=F32)
    n_ref[...] = dec * n_ref[...] + jnp.sum(kcw, axis=0, keepdims=True)
    m_ref[...] = m_new

    if emit_state:
        @pl.when(c == nc - 1)
        def _():
            co_ref[0, 0, 0] = c_ref[...]
            no_ref[0, 0, 0] = n_ref[...]
            mo_ref[0, 0, 0] = m_ref[...]


def _mlpost_kernel(h_ref, xc_ref, z_ref, gn_ref, sk_ref, o_ref):
    o = _rms(h_ref[0] + h_ref[1]) * gn_ref[...]
    o = (o + sk_ref[...] * xc_ref[...].astype(F32)) * _silu(z_ref[...].astype(F32))
    o_ref[...] = o.astype(BF16)


def _mlstm_layer(x2, B, L, p, mods, rows_per_mod, ctx):
    shift, scale, gate = mods
    H, dk = p['gn_g'].shape
    Wd = H * dk
    bsz = p['qkv_bd'].shape[-1]
    nc = L // CHUNK
    rows = x2.shape[0]
    has_s0 = ctx is not None
    emit_state = not has_s0
    u = _inproj(x2, p['norm_g'], scale, shift, p['in_w'].astype(BF16), rows_per_mod, BF16)

    bw = 256 if Wd % 256 == 0 else LANES
    nb = Wd // bw
    per = bw // bsz
    bd = p['qkv_bd'].reshape(3, nb, per, bsz, bsz)
    eye = jnp.eye(per, dtype=F32)
    wbd = jnp.einsum('tngio,gh->tngiho', bd, eye).reshape(3, nb, bw, bw).astype(BF16)
    n_gate = p['gate_w'].shape[-1]
    gw = p['gate_w'].reshape(2, 3, Wd, n_gate).transpose(1, 2, 0, 3).reshape(3, Wd, 2 * n_gate)
    gw = jnp.pad(gw, ((0, 0), (0, 0), (0, LANES - 2 * n_gate))).astype(BF16)

    tc = bw
    nbt = tc // bw
    xc, q, k, v, gates = pl.pallas_call(
        functools.partial(_mlpre_kernel, bw=bw),
        grid=(B, Wd // tc),
        in_specs=[pl.BlockSpec((L, tc), lambda b, j: (b, j)),
                  pl.BlockSpec((p['conv_w'].shape[0], tc), lambda b, j: (0, j)),
                  pl.BlockSpec((1, tc), lambda b, j: (0, j)),
                  pl.BlockSpec((3, nbt, bw, bw), lambda b, j: (0, j, 0, 0)),
                  pl.BlockSpec((3, tc, LANES), lambda b, j: (0, j, 0))],
        out_specs=[pl.BlockSpec((L, tc), lambda b, j: (b, j))] * 4
                  + [pl.BlockSpec((L, LANES), lambda b, j: (b, 0))],
        out_shape=[jax.ShapeDtypeStruct((rows, Wd), BF16)] * 4
                  + [jax.ShapeDtypeStruct((rows, LANES), F32)],
        compiler_params=_params("parallel", "arbitrary"),
        name="mlstm_pre",
    )(u, p['conv_w'], p['conv_b'].reshape(1, Wd), wbd, gw)

    gl = gates[:, :2 * n_gate].reshape(B, nc, CHUNK, 2, 2, H).transpose(0, 3, 5, 1, 4, 2)
    gb = p['gate_b'].reshape(2, 2, H).transpose(0, 2, 1).reshape(2, H, 2, 1)

    def chunk_of(d, c):
        return c + d * (nc - 1 - 2 * c)

    tok = lambda b, h, d, c: (b * nc + chunk_of(d, c), h)
    in_specs = [pl.BlockSpec((1, 1, 1, 1, 2, CHUNK), lambda b, h, d, c: (b, d, h, chunk_of(d, c), 0, 0)),
                pl.BlockSpec((1, 1, 2, 1), lambda b, h, d, c: (d, h, 0, 0)),
                pl.BlockSpec((CHUNK, dk), tok),
                pl.BlockSpec((CHUNK, dk), tok),
                pl.BlockSpec((CHUNK, dk), tok)]
    args = [gl, gb, q, k, v]
    c_spec = pl.BlockSpec((1, 1, 1, dk, dk), lambda b, h, d, c: (b, d, h, 0, 0))
    n_spec = pl.BlockSpec((1, 1, 1, 1, dk), lambda b, h, d, c: (b, d, h, 0, 0))
    m_spec = pl.BlockSpec((1, 1, 1, 1, 1), lambda b, h, d, c: (b, d, h, 0, 0))
    if has_s0:
        c0, n0, m0 = ctx
        in_specs += [c_spec, n_spec, m_spec]
        args += [c0, n0.reshape(B, 2, H, 1, dk), m0.reshape(B, 2, H, 1, 1)]
    out_specs = [pl.BlockSpec((1, CHUNK, dk), lambda b, h, d, c: (d, b * nc + chunk_of(d, c), h))]
    out_shape = [jax.ShapeDtypeStruct((2, rows, Wd), F32)]
    if emit_state:
        out_specs += [c_spec, n_spec, m_spec]
        out_shape += [jax.ShapeDtypeStruct((B, 2, H, dk, dk), F32),
                      jax.ShapeDtypeStruct((B, 2, H, 1, dk), F32),
                      jax.ShapeDtypeStruct((B, 2, H, 1, 1), F32)]
    res = pl.pallas_call(
        functools.partial(_mlscan_kernel, nc=nc, dk=dk, has_s0=has_s0, emit_state=emit_state),
        grid=(B, H, 2, nc),
        in_specs=in_specs,
        out_specs=out_specs,
        out_shape=out_shape,
        scratch_shapes=[pltpu.VMEM((dk, dk), F32), pltpu.VMEM((1, dk), F32), pltpu.VMEM((1, 1), F32)],
        compiler_params=_params("parallel", "parallel", "arbitrary", "arbitrary"),
        name="mlstm_scan",
    )(*args)
    hdir = res[0]

    tm = _tile(rows, 256)
    o = pl.pallas_call(
        _mlpost_kernel,
        grid=(rows // tm, H),
        in_specs=[pl.BlockSpec((2, tm, dk), lambda i, h: (0, i, h)),
                  pl.BlockSpec((tm, dk), lambda i, h: (i, h)),
                  pl.BlockSpec((tm, dk), lambda i, h: (i, H + h)),
                  pl.BlockSpec((1, dk), lambda i, h: (0, h)),
                  pl.BlockSpec((1, dk), lambda i, h: (0, h))],
        out_specs=pl.BlockSpec((tm, dk), lambda i, h: (i, h)),
        out_shape=jax.ShapeDtypeStruct((rows, Wd), BF16),
        compiler_params=_params("parallel", "arbitrary"),
        name="mlstm_post",
    )(hdir, xc, u, p['gn_g'].reshape(1, Wd), p['skip'].reshape(1, Wd))
    y = _outproj(o, p['out_w'].astype(BF16), x2, gate, rows_per_mod)
    new = None
    if emit_state:
        new = (res[1], res[2].reshape(B, 2, H, dk), res[3].reshape(B, 2, H))
    return y, new


def kernel(x_prompt, x_sample, c, c_ctx, cache_k_l0, cache_v_l0, state_ret_l1, state_C_l2, state_n_l2, state_m_l2, cache_k_l3, cache_v_l3, norm_g_l0, ada_w_l0, ada_b_l0, in_w_l0, out_w_l0, qk_g_l0, lam_l0, subln_g_l0, norm_g_l1, ada_w_l1, ada_b_l1, in_w_l1, out_w_l1, decay_l1, gn_g_l1, norm_g_l2, ada_w_l2, ada_b_l2, in_w_l2, out_w_l2, conv_w_l2, conv_b_l2, qkv_bd_l2, gate_w_l2, gate_b_l2, gn_g_l2, skip_l2, norm_g_l3, ada_w_l3, ada_b_l3, in_w_l3, out_w_l3, qk_g_l3, lam_l3, subln_g_l3):
    layers = [
        dict(norm_g=norm_g_l0, ada_w=ada_w_l0, ada_b=ada_b_l0, in_w=in_w_l0, out_w=out_w_l0,
             qk_g=qk_g_l0, lam=lam_l0, subln_g=subln_g_l0),
        dict(norm_g=norm_g_l1, ada_w=ada_w_l1, ada_b=ada_b_l1, in_w=in_w_l1, out_w=out_w_l1,
             decay=decay_l1, gn_g=gn_g_l1),
        dict(norm_g=norm_g_l2, ada_w=ada_w_l2, ada_b=ada_b_l2, in_w=in_w_l2, out_w=out_w_l2,
             conv_w=conv_w_l2, conv_b=conv_b_l2, qkv_bd=qkv_bd_l2, gate_w=gate_w_l2, gate_b=gate_b_l2,
             gn_g=gn_g_l2, skip=skip_l2),
        dict(norm_g=norm_g_l3, ada_w=ada_w_l3, ada_b=ada_b_l3, in_w=in_w_l3, out_w=out_w_l3,
             qk_g=qk_g_l3, lam=lam_l3, subln_g=subln_g_l3),
    ]
    ctxs = [(cache_k_l0, cache_v_l0), (state_ret_l1,), (state_C_l2, state_n_l2, state_m_l2),
            (cache_k_l3, cache_v_l3)]
    Bp, Lp, D = x_prompt.shape
    Bs, Ls, _ = x_sample.shape
    xp = x_prompt.reshape(Bp * Lp, D)
    xs = x_sample.reshape(Bs * Ls, D)
    n_cond = 1 + Bs
    cond_rows = -(-n_cond // 8) * 8
    cond = jnp.concatenate([c_ctx[None, :], c, jnp.zeros((cond_rows - n_cond, D), F32)], axis=0)
    new_states = []
    for i, p in enumerate(layers):
        m = _adaln(cond, p['ada_w'], p['ada_b'])
        mods_p = tuple(m[0:1, k * D:(k + 1) * D].reshape(1, 1, D) for k in range(3))
        mods_s = tuple(m[1:n_cond, k * D:(k + 1) * D].reshape(Bs, 1, D) for k in range(3))
        kind = i % 3
        if kind == 0:
            xp, st = _diff_attention_layer(xp, Bp, Lp, p, mods_p, Bp * Lp, i, None)
            xs, _ = _diff_attention_layer(xs, Bs, Ls, p, mods_s, Ls, i, ctxs[i])
        elif kind == 1:
            xp, st = _retention_layer(xp, Bp, Lp, p, mods_p, Bp * Lp, None)
            xs, _ = _retention_layer(xs, Bs, Ls, p, mods_s, Ls, ctxs[i])
        else:
            xp, st = _mlstm_layer(xp, Bp, Lp, p, mods_p, Bp * Lp, None)
            xs, _ = _mlstm_layer(xs, Bs, Ls, p, mods_s, Ls, ctxs[i])
        new_states.append(st)
    (k_l0, v_l0), (ret_l1,), (C_l2, n_l2, m_l2), (k_l3, v_l3) = new_states
    return (xp.reshape(Bp, Lp, D), xs.reshape(Bs, Ls, D), k_l0, v_l0, ret_l1, C_l2, n_l2, m_l2, k_l3, v_l3)
```

```python
import functools
import math

import jax
import jax.numpy as jnp
from jax import lax
from jax.experimental import pallas as pl
from jax.experimental.pallas import tpu as pltpu

F32 = jnp.float32
BF16 = jnp.bfloat16

CHUNK = 256
GRID_W = 64
ROPE_BASE = 10000.0
NORM_EPS = 1e-6
LANES = 128
VMEM_LIMIT_BYTES = 56 * 1024 * 1024

LOG2E = math.log2(math.e)

NT_DIMS = (((1,), (1,)), ((), ()))
TN_DIMS = (((0,), (0,)), ((), ()))


def _tile(n, pref):
    if n <= pref:
        return n
    t = pref
    while n % t:
        t -= 1
    return t


def _params(*sem):
    return pltpu.CompilerParams(dimension_semantics=sem, vmem_limit_bytes=VMEM_LIMIT_BYTES)


def _silu(x):
    return x * jax.nn.sigmoid(x)


def _rms(x):
    return x * lax.rsqrt(jnp.mean(x * x, axis=-1, keepdims=True) + NORM_EPS)


def _adaln_kernel(c_ref, w_ref, b_ref, o_ref):
    s = _silu(c_ref[...]).astype(BF16)
    o_ref[...] = jnp.dot(s, w_ref[...].astype(BF16), preferred_element_type=F32) + b_ref[...]


def _adaln(cond, w, b):
    R, D = cond.shape
    N = w.shape[1]
    tn = _tile(N, 512)
    return pl.pallas_call(
        _adaln_kernel,
        grid=(N // tn,),
        in_specs=[pl.BlockSpec((R, D), lambda j: (0, 0)),
                  pl.BlockSpec((D, tn), lambda j: (0, j)),
                  pl.BlockSpec((1, tn), lambda j: (0, j))],
        out_specs=pl.BlockSpec((R, tn), lambda j: (0, j)),
        out_shape=jax.ShapeDtypeStruct((R, N), F32),
        compiler_params=_params("arbitrary"),
        name="adaln",
    )(cond, w, b.reshape(1, N))


def _inproj_kernel(x_ref, g_ref, sc_ref, sh_ref, w_ref, o_ref, h_ref):
    @pl.when(pl.program_id(1) == 0)
    def _():
        gain = g_ref[...] * (1.0 + sc_ref[0])
        h_ref[...] = (_rms(x_ref[...]) * gain + sh_ref[0]).astype(BF16)

    o_ref[...] = jnp.dot(h_ref[...], w_ref[...], preferred_element_type=F32).astype(o_ref.dtype)


def _inproj(x2, norm_g, scale, shift, w, rows_per_mod, out_dtype, tm_pref=1024, tn_pref=512):
    rows, D = x2.shape
    N = w.shape[1]
    tm = _tile(rows_per_mod, tm_pref)
    tn = _tile(N, tn_pref)
    mod = lambda i, j: ((i * tm) // rows_per_mod, 0, 0)
    return pl.pallas_call(
        _inproj_kernel,
        grid=(rows // tm, N // tn),
        in_specs=[pl.BlockSpec((tm, D), lambda i, j: (i, 0)),
                  pl.BlockSpec((1, D), lambda i, j: (0, 0)),
                  pl.BlockSpec((1, 1, D), mod),
                  pl.BlockSpec((1, 1, D), mod),
                  pl.BlockSpec((D, tn), lambda i, j: (0, j))],
        out_specs=pl.BlockSpec((tm, tn), lambda i, j: (i, j)),
        out_shape=jax.ShapeDtypeStruct((rows, N), out_dtype),
        scratch_shapes=[pltpu.VMEM((tm, D), BF16)],
        compiler_params=_params("parallel", "arbitrary"),
        name="inproj",
    )(x2, norm_g.reshape(1, D), scale, shift, w)


def _outproj_kernel(a_ref, w_ref, x_ref, gt_ref, o_ref):
    y = jnp.dot(a_ref[...], w_ref[...], preferred_element_type=F32)
    o_ref[...] = x_ref[...] + gt_ref[0] * y


def _outproj(a, w, x2, gate, rows_per_mod, tm_pref=1024, tn_pref=512):
    rows, K = a.shape
    D = w.shape[1]
    tm = _tile(rows_per_mod, tm_pref)
    tn = _tile(D, tn_pref)
    w_mode = dict(pipeline_mode=pl.Buffered(1)) if tn == D else {}
    return pl.pallas_call(
        _outproj_kernel,
        grid=(rows // tm, D // tn),
        in_specs=[pl.BlockSpec((tm, K), lambda i, j: (i, 0)),
                  pl.BlockSpec((K, tn), lambda i, j: (0, j), **w_mode),
                  pl.BlockSpec((tm, tn), lambda i, j: (i, j)),
                  pl.BlockSpec((1, 1, tn), lambda i, j: ((i * tm) // rows_per_mod, 0, j))],
        out_specs=pl.BlockSpec((tm, tn), lambda i, j: (i, j)),
        out_shape=jax.ShapeDtypeStruct((rows, D), F32),
        compiler_params=_params("parallel", "arbitrary"),
        name="outproj",
    )(a, w, x2, gate)


def _rope_tables(L, dh):
    t = jnp.arange(L)
    row = (t // GRID_W).astype(F32)
    col = (t % GRID_W).astype(F32)
    n_freq = dh // 4
    inv = ROPE_BASE ** (-jnp.arange(n_freq, dtype=F32) / n_freq)
    ar = row[:, None] * inv[None, :]
    ac = col[:, None] * inv[None, :]
    cos = jnp.concatenate([jnp.cos(ar), jnp.cos(ar), jnp.cos(ac), jnp.cos(ac)], axis=-1)
    sin = jnp.concatenate([-jnp.sin(ar), jnp.sin(ar), -jnp.sin(ac), jnp.sin(ac)], axis=-1)
    return cos.astype(F32), sin.astype(F32)


def _qkpost_kernel(*refs, n_sub, dh, rope, emit_k):
    if rope:
        u_ref, g_ref, cos_ref, sin_ref = refs[:4]
        outs = refs[4:]
    else:
        u_ref, g_ref = refs[:2]
        outs = refs[2:]
    o_ref = outs[0]
    g = g_ref[0]
    factor = jnp.where(pl.program_id(1) == 0, dh ** -0.5 * LOG2E, 1.0)
    if rope:
        cos = cos_ref[...]
        sin = sin_ref[...]
        lane = lax.broadcasted_iota(jnp.int32, cos.shape, 1)
        first = (lane % (dh // 2)) < (dh // 4)
    for s in range(n_sub):
        sl = slice(s * dh, (s + 1) * dh)
        y = _rms(u_ref[:, sl].astype(F32)) * g
        if rope:
            swapped = jnp.where(first, pltpu.roll(y, dh - dh // 4, 1), pltpu.roll(y, dh // 4, 1))
            y = y * cos + swapped * sin
        o_ref[:, sl] = (y * factor).astype(BF16)
        if emit_k:
            outs[1][:, sl] = y


def _qkpost(u, qk_g, L, n_sub, dh, rope, emit_k):
    rows = u.shape[0]
    W = n_sub * dh
    tm = _tile(L, 512)
    nl = L // tm
    in_specs = [pl.BlockSpec((tm, W), lambda i, j: (i, j)),
                pl.BlockSpec((1, 1, dh), lambda i, j: (j, 0, 0))]
    args = [u, qk_g.reshape(2, 1, dh)]
    if rope:
        cos, sin = _rope_tables(L, dh)
        in_specs += [pl.BlockSpec((tm, dh), lambda i, j: (i % nl, 0))] * 2
        args += [cos, sin]
    out_specs = [pl.BlockSpec((tm, W), lambda i, j: (i, j))]
    out_shape = [jax.ShapeDtypeStruct((rows, 2 * W), BF16)]
    if emit_k:
        out_specs.append(pl.BlockSpec((tm, W), lambda i, j: (i, 0)))
        out_shape.append(jax.ShapeDtypeStruct((rows, W), F32))
    return pl.pallas_call(
        functools.partial(_qkpost_kernel, n_sub=n_sub, dh=dh, rope=rope, emit_k=emit_k),
        grid=(rows // tm, 2),
        in_specs=in_specs,
        out_specs=out_specs,
        out_shape=out_shape,
        compiler_params=_params("parallel", "arbitrary"),
        name="qkpost",
    )(*args)


def _attn_kernel(*refs, dh, lam_init, has_ctx):
    if has_ctx:
        lam_ref, sg_ref, q_ref, k_ref, v_ref, g_ref, ck_ref, cv_ref, o_ref = refs
    else:
        lam_ref, sg_ref, q_ref, k_ref, v_ref, g_ref, o_ref = refs
    lm = lam_ref[...]
    lam = (jnp.exp(jnp.sum(lm[0:1] * lm[1:2], axis=1, keepdims=True))
           - jnp.exp(jnp.sum(lm[2:3] * lm[3:4], axis=1, keepdims=True)) + lam_init)
    v = v_ref[...].astype(BF16)
    o = None
    for c in range(2):
        sl = slice(c * dh, (c + 1) * dh)
        qc = q_ref[:, sl]
        s1 = lax.dot_general(qc, k_ref[:, sl], NT_DIMS, preferred_element_type=F32)
        m = jnp.max(s1, axis=-1, keepdims=True)
        if has_ctx:
            s2 = lax.dot_general(qc, ck_ref[0, :, sl], NT_DIMS, preferred_element_type=F32)
            m = jnp.maximum(m, jnp.max(s2, axis=-1, keepdims=True))
        p1 = jnp.exp2(s1 - m)
        l = jnp.sum(p1, axis=-1, keepdims=True)
        oc = jnp.dot(p1.astype(BF16), v, preferred_element_type=F32)
        if has_ctx:
            p2 = jnp.exp2(s2 - m)
            l = l + jnp.sum(p2, axis=-1, keepdims=True)
            oc = oc + jnp.dot(p2.astype(BF16), cv_ref[0], preferred_element_type=F32)
        coef = (1.0 / l) if c == 0 else (-lam / l)
        o = oc * coef if o is None else o + oc * coef
    o = _rms(o) * sg_ref[...] * (1.0 - lam_init)
    o_ref[...] = (o * _silu(g_ref[...].astype(F32))).astype(BF16)


def _attention(qk, u, lam_p, subln_g, ctx, B, L, H, dh, lam_init, tq_pref):
    rows = qk.shape[0]
    V = 2 * dh
    tq = _tile(L, tq_pref)
    nq = L // tq
    has_ctx = ctx is not None
    in_specs = [pl.BlockSpec((4, dh), lambda b, h, i: (0, 0)),
                pl.BlockSpec((1, V), lambda b, h, i: (0, 0)),
                pl.BlockSpec((tq, V), lambda b, h, i: (b * nq + i, h)),
                pl.BlockSpec((L, V), lambda b, h, i: (b, H + h)),
                pl.BlockSpec((L, V), lambda b, h, i: (b, 2 * H + h)),
                pl.BlockSpec((tq, V), lambda b, h, i: (b * nq + i, 3 * H + h))]
    args = [lam_p, subln_g.reshape(1, V), qk, qk, u, u]
    if has_ctx:
        P = ctx[0].shape[1]
        in_specs += [pl.BlockSpec((1, P, V), lambda b, h, i: (b, 0, h))] * 2
        args += list(ctx)
    return pl.pallas_call(
        functools.partial(_attn_kernel, dh=dh, lam_init=lam_init, has_ctx=has_ctx),
        grid=(B, H, nq),
        in_specs=in_specs,
        out_specs=pl.BlockSpec((tq, V), lambda b, h, i: (b * nq + i, h)),
        out_shape=jax.ShapeDtypeStruct((rows, H * V), BF16),
        compiler_params=_params("parallel", "parallel", "arbitrary"),
        name="diffattn",
    )(*args)


def _diff_attention_layer(x2, B, L, p, mods, rows_per_mod, layer_idx, ctx, tiles):
    shift, scale, gate = mods
    dh = p['qk_g'].shape[1]
    V = p['subln_g'].shape[0]
    QKW = (p['in_w'].shape[1] - 2 * p['out_w'].shape[0]) // 2
    H = p['out_w'].shape[0] // V
    n_sub = QKW // dh
    lam_init = 0.8 - 0.6 * math.exp(-0.3 * layer_idx)
    is_prompt = ctx is None
    u = _inproj(x2, p['norm_g'], scale, shift, p['in_w'].astype(BF16), rows_per_mod,
                F32 if is_prompt else BF16, *tiles['inproj'])
    res = _qkpost(u, p['qk_g'], L, n_sub, dh, rope=not is_prompt, emit_k=is_prompt)
    new = None
    if is_prompt:
        qk, kf = res
        new = (kf.reshape(B, L, n_sub, dh), u[:, 2 * QKW:2 * QKW + H * V].reshape(B, L, H, V))
        cache = None
    else:
        qk = res[0]
        ck, cv = ctx
        P = ck.shape[1]
        cache = (ck.reshape(B, P, QKW).astype(BF16), cv.reshape(B, P, H * V).astype(BF16))
    o = _attention(qk, u, p['lam'], p['subln_g'], cache, B, L, H, dh, lam_init, tiles['tq'])
    y = _outproj(o, p['out_w'].astype(BF16), x2, gate, rows_per_mod, *tiles['outproj'])
    return y, new


def _ret_kernel(*refs, nc, dk, has_s0, emit_state):
    dec_ref, gn_ref, q_ref, k_ref, v_ref, g_ref = refs[:6]
    i = 6
    if has_s0:
        s0_ref = refs[i]
        i += 1
    o_ref = refs[i]
    i += 1
    if emit_state:
        so_ref = refs[i]
        i += 1
    sf_ref, sb_ref, acc_ref = refs[i:]
    C = CHUNK
    lg = jnp.log1p(-jnp.exp(dec_ref[0]))
    lgf = lg[0:1]
    lgb = lg[1:2]
    ii = lax.broadcasted_iota(jnp.int32, (C, C), 0)
    jj = lax.broadcasted_iota(jnp.int32, (C, C), 1)
    rel = (ii - jj).astype(F32)
    dtot = (jnp.where(rel >= 0, jnp.exp(jnp.maximum(rel, 0.0) * lgf), 0.0)
            + jnp.where(rel <= 0, jnp.exp(jnp.maximum(-rel, 0.0) * lgb), 0.0))
    pos = lax.broadcasted_iota(jnp.int32, (C, 1), 0).astype(F32)
    qdf = jnp.exp((pos + 1.0) * lgf)
    kdf = jnp.exp((C - 1.0 - pos) * lgf)
    qdb = jnp.exp((C - pos) * lgb)
    kdb = jnp.exp(pos * lgb)
    cdf = jnp.exp(C * lgf)
    cdb = jnp.exp(C * lgb)
    scale = dk ** -0.5
    if has_s0:
        sf_ref[...] = s0_ref[0, 0, 0]
        sb_ref[...] = s0_ref[0, 1, 0]
    else:
        sf_ref[...] = jnp.zeros_like(sf_ref)
        sb_ref[...] = jnp.zeros_like(sb_ref)
    gn = gn_ref[0]

    def load(c):
        r = pl.multiple_of(c * C, C)
        q = q_ref[pl.ds(r, C), :]
        k = k_ref[pl.ds(r, C), :].astype(F32) * scale
        v = v_ref[pl.ds(r, C), :].astype(BF16)
        return r, q, k, v

    def fwd(c, carry):
        r, q, k, v = load(c)
        att = lax.dot_general(q, k.astype(BF16), NT_DIMS, preferred_element_type=F32) * dtot
        o = jnp.dot(att.astype(BF16), v, preferred_element_type=F32)
        o = o + jnp.dot((q.astype(F32) * qdf).astype(BF16), sf_ref[...].astype(BF16),
                        preferred_element_type=F32)
        acc_ref[pl.ds(r, C), :] = o
        sf_ref[...] = cdf * sf_ref[...] + lax.dot_general(
            (k * kdf).astype(BF16), v, TN_DIMS, preferred_element_type=F32)
        return carry

    lax.fori_loop(0, nc, fwd, 0)

    def bwd(t, carry):
        r, q, k, v = load(nc - 1 - t)
        o = acc_ref[pl.ds(r, C), :] + jnp.dot(
            (q.astype(F32) * qdb).astype(BF16), sb_ref[...].astype(BF16), preferred_element_type=F32)
        sb_ref[...] = cdb * sb_ref[...] + lax.dot_general(
            (k * kdb).astype(BF16), v, TN_DIMS, preferred_element_type=F32)
        g = g_ref[pl.ds(r, C), :].astype(F32)
        o_ref[pl.ds(r, C), :] = (_rms(o) * gn * _silu(g)).astype(BF16)
        return carry

    lax.fori_loop(0, nc, bwd, 0)
    if emit_state:
        so_ref[0, 0, 0] = sf_ref[...]
        so_ref[0, 1, 0] = sb_ref[...]


def _retention_layer(x2, B, L, p, mods, rows_per_mod, ctx, tiles):
    shift, scale, gate = mods
    H, dv = p['gn_g'].shape
    dk = (p['in_w'].shape[1] - 2 * H * dv) // (2 * H)
    nc = L // CHUNK
    rows = x2.shape[0]
    has_s0 = ctx is not None
    emit_state = not has_s0
    u = _inproj(x2, p['norm_g'], scale, shift, p['in_w'].astype(BF16), rows_per_mod, BF16,
                *tiles['inproj'])
    kq = (H * dk) // dk
    kv = (2 * H * dk) // dv
    in_specs = [pl.BlockSpec((1, 2, 1), lambda b, h: (h, 0, 0)),
                pl.BlockSpec((1, 1, dv), lambda b, h: (h, 0, 0)),
                pl.BlockSpec((L, dk), lambda b, h: (b, h)),
                pl.BlockSpec((L, dk), lambda b, h: (b, kq + h)),
                pl.BlockSpec((L, dv), lambda b, h: (b, kv + h)),
                pl.BlockSpec((L, dv), lambda b, h: (b, kv + H + h))]
    args = [p['decay'].T.reshape(H, 2, 1), p['gn_g'].reshape(H, 1, dv), u, u, u, u]
    st_spec = pl.BlockSpec((1, 2, 1, dk, dv), lambda b, h: (b, 0, h, 0, 0))
    if has_s0:
        in_specs.append(st_spec)
        args.append(ctx[0])
    out_specs = [pl.BlockSpec((L, dv), lambda b, h: (b, h))]
    out_shape = [jax.ShapeDtypeStruct((rows, H * dv), BF16)]
    if emit_state:
        out_specs.append(st_spec)
        out_shape.append(jax.ShapeDtypeStruct((B, 2, H, dk, dv), F32))
    res = pl.pallas_call(
        functools.partial(_ret_kernel, nc=nc, dk=dk, has_s0=has_s0, emit_state=emit_state),
        grid=(B, H),
        in_specs=in_specs,
        out_specs=out_specs,
        out_shape=out_shape,
        scratch_shapes=[pltpu.VMEM((dk, dv), F32), pltpu.VMEM((dk, dv), F32), pltpu.VMEM((L, dv), F32)],
        compiler_params=_params("parallel", "parallel"),
        name="retention",
    )(*args)
    y = _outproj(res[0], p['out_w'].astype(BF16), x2, gate, rows_per_mod, *tiles['outproj'])
    return y, ((res[1],) if emit_state else None)


def _mlpre_kernel(xm_ref, cw_ref, cb_ref, wbd_ref, gw_ref, xc_ref, q_ref, k_ref, v_ref, gt_ref, *, bw):
    L, tc = xm_ref.shape
    x = xm_ref[...].astype(F32)
    cw = cw_ref[...]
    W = cw.shape[0]
    pad = (W - 1) // 2
    row = lax.broadcasted_iota(jnp.int32, (L, tc), 0)
    acc = x * cw[pad:pad + 1] + cb_ref[...]
    for w in range(W):
        d = w - pad
        if d == 0:
            continue
        sh = pltpu.roll(x, (-d) % L, 0)
        valid = (row < L - d) if d > 0 else (row >= -d)
        acc = acc + jnp.where(valid, sh, 0.0) * cw[w:w + 1]
    xc = _silu(acc)
    xc_ref[...] = xc.astype(BF16)
    xcb = xc.astype(BF16)
    xmb = xm_ref[...].astype(BF16)
    gsum = jnp.zeros(gt_ref.shape, F32)
    for blk in range(tc // bw):
        sl = slice(blk * bw, (blk + 1) * bw)
        q = jnp.dot(xcb[:, sl], wbd_ref[0, blk], preferred_element_type=F32).astype(BF16)
        k = jnp.dot(xcb[:, sl], wbd_ref[1, blk], preferred_element_type=F32).astype(BF16)
        v = jnp.dot(xmb[:, sl], wbd_ref[2, blk], preferred_element_type=F32).astype(BF16)
        q_ref[:, sl] = q
        k_ref[:, sl] = k
        v_ref[:, sl] = v
        gsum = gsum + jnp.dot(q, gw_ref[0, sl, :], preferred_element_type=F32)
        gsum = gsum + jnp.dot(k, gw_ref[1, sl, :], preferred_element_type=F32)
        gsum = gsum + jnp.dot(v, gw_ref[2, sl, :], preferred_element_type=F32)

    @pl.when(pl.program_id(1) == 0)
    def _():
        gt_ref[...] = gsum

    @pl.when(pl.program_id(1) != 0)
    def _():
        gt_ref[...] = gt_ref[...] + gsum


def _logsigmoid(x):
    return jnp.minimum(x, 0.0) - jnp.log1p(jnp.exp(-jnp.abs(x)))


def _mlscan_kernel(*refs, nc, dk, has_s0, emit_state):
    gl_ref, gb_ref, q_ref, k_ref, v_ref = refs[:5]
    i = 5
    if has_s0:
        c0_ref, n0_ref, m0_ref = refs[i:i + 3]
        i += 3
    h_ref = refs[i]
    i += 1
    if emit_state:
        co_ref, no_ref, mo_ref = refs[i:i + 3]
        i += 3
    c_ref, n_ref, m_ref = refs[i:]
    C = CHUNK
    d = pl.program_id(2)
    c = pl.program_id(3)

    @pl.when(c == 0)
    def _():
        if has_s0:
            c_ref[...] = c0_ref[0, 0, 0]
            n_ref[...] = n0_ref[0, 0, 0]
            m_ref[...] = m0_ref[0, 0, 0]
        else:
            c_ref[...] = jnp.zeros_like(c_ref)
            n_ref[...] = jnp.zeros_like(n_ref)
            m_ref[...] = jnp.zeros_like(m_ref)

    gl = gl_ref[0, 0, 0, 0] + gb_ref[0, 0]
    ih_row = gl[0:1]
    f_row = _logsigmoid(gl[1:2])
    ii = lax.broadcasted_iota(jnp.int32, (C, C), 0)
    jj = lax.broadcasted_iota(jnp.int32, (C, C), 1)
    sgn = 1 - 2 * d
    seen = (ii - jj) * sgn >= 0
    seen_t = (jj - ii) * sgn >= 0
    eye = ii == jj
    fmat = jnp.broadcast_to(f_row, (C, C))
    imat = jnp.broadcast_to(ih_row, (C, C))
    bh_col = jnp.sum(jnp.where(seen, fmat, 0.0), axis=1, keepdims=True)
    f_col = jnp.sum(jnp.where(eye, fmat, 0.0), axis=1, keepdims=True)
    ih_col = jnp.sum(jnp.where(eye, imat, 0.0), axis=1, keepdims=True)
    bh_row = jnp.sum(jnp.where(seen_t, jnp.broadcast_to(f_col, (C, C)), 0.0), axis=0, keepdims=True)
    tot = jnp.sum(f_row, axis=1, keepdims=True)
    m_prev = m_ref[...]
    dlog = jnp.where(seen, bh_col - bh_row + ih_row, -jnp.inf)
    inter = bh_col + m_prev
    m_i = jnp.maximum(jnp.max(dlog, axis=1, keepdims=True), inter)
    w = jnp.exp(dlog - m_i)
    wi = jnp.exp(inter - m_i)
    q = q_ref[...]
    kf = k_ref[...].astype(F32) * (dk ** -0.5)
    v = v_ref[...]
    s = lax.dot_general(q, kf.astype(BF16), NT_DIMS, preferred_element_type=F32) * w
    num = jnp.dot(s.astype(BF16), v, preferred_element_type=F32)
    num = num + wi * jnp.dot(q, c_ref[...].astype(BF16), preferred_element_type=F32)
    qn = jnp.sum(q.astype(F32) * n_ref[...], axis=1, keepdims=True)
    den = jnp.sum(s, axis=1, keepdims=True) + wi * qn
    norm = jnp.maximum(jnp.abs(den), jnp.exp(-m_i))
    h_ref[0] = (num / norm).astype(h_ref.dtype)
    kw = tot - bh_col + ih_col
    m_new = jnp.maximum(tot + m_prev, jnp.max(kw, axis=0, keepdims=True))
    wk = jnp.exp(kw - m_new)
    dec = jnp.exp(tot + m_prev - m_new)
    kcw = kf * wk
    c_ref[...] = dec * c_ref[...] + lax.dot_general(kcw.astype(BF16), v, TN_DIMS,
                                                   preferred_element_type=F32)
    n_ref[...] = dec * n_ref[...] + jnp.sum(kcw, axis=0, keepdims=True)
    m_ref[...] = m_new

    if emit_state:
        @pl.when(c == nc - 1)
        def _():
            co_ref[0, 0, 0] = c_ref[...]
            no_ref[0, 0, 0] = n_ref[...]
            mo_ref[0, 0, 0] = m_ref[...]


def _mlpost_kernel(h_ref, xc_ref, z_ref, gn_ref, sk_ref, o_ref):
    o = _rms(h_ref[0].astype(F32) + h_ref[1].astype(F32)) * gn_ref[...]
    o = (o + sk_ref[...] * xc_ref[...].astype(F32)) * _silu(z_ref[...].astype(F32))
    o_ref[...] = o.astype(BF16)


def _mlstm_layer(x2, B, L, p, mods, rows_per_mod, ctx, tiles):
    shift, scale, gate = mods
    H, dk = p['gn_g'].shape
    Wd = H * dk
    bsz = p['qkv_bd'].shape[-1]
    nc = L // CHUNK
    rows = x2.shape[0]
    has_s0 = ctx is not None
    emit_state = not has_s0
    u = _inproj(x2, p['norm_g'], scale, shift, p['in_w'].astype(BF16), rows_per_mod, BF16,
                *tiles['inproj'])

    bw = 256 if Wd % 256 == 0 else LANES
    nb = Wd // bw
    per = bw // bsz
    bd = p['qkv_bd'].reshape(3, nb, per, bsz, bsz)
    eye = jnp.eye(per, dtype=F32)
    wbd = jnp.einsum('tngio,gh->tngiho', bd, eye).reshape(3, nb, bw, bw).astype(BF16)
    n_gate = p['gate_w'].shape[-1]
    gw = p['gate_w'].reshape(2, 3, Wd, n_gate).transpose(1, 2, 0, 3).reshape(3, Wd, 2 * n_gate)
    gw = jnp.pad(gw, ((0, 0), (0, 0), (0, LANES - 2 * n_gate))).astype(BF16)

    tc = bw
    nbt = tc // bw
    xc, q, k, v, gates = pl.pallas_call(
        functools.partial(_mlpre_kernel, bw=bw),
        grid=(B, Wd // tc),
        in_specs=[pl.BlockSpec((L, tc), lambda b, j: (b, j)),
                  pl.BlockSpec((p['conv_w'].shape[0], tc), lambda b, j: (0, j)),
                  pl.BlockSpec((1, tc), lambda b, j: (0, j)),
                  pl.BlockSpec((3, nbt, bw, bw), lambda b, j: (0, j, 0, 0)),
                  pl.BlockSpec((3, tc, LANES), lambda b, j: (0, j, 0))],
        out_specs=[pl.BlockSpec((L, tc), lambda b, j: (b, j))] * 4
                  + [pl.BlockSpec((L, LANES), lambda b, j: (b, 0))],
        out_shape=[jax.ShapeDtypeStruct((rows, Wd), BF16)] * 4
                  + [jax.ShapeDtypeStruct((rows, LANES), F32)],
        compiler_params=_params("parallel", "arbitrary"),
        name="mlstm_pre",
    )(u, p['conv_w'], p['conv_b'].reshape(1, Wd), wbd, gw)

    gl = gates[:, :2 * n_gate].reshape(B, nc, CHUNK, 2, 2, H).transpose(0, 3, 5, 1, 4, 2)
    gb = p['gate_b'].reshape(2, 2, H).transpose(0, 2, 1).reshape(2, H, 2, 1)

    def chunk_of(d, c):
        return c + d * (nc - 1 - 2 * c)

    tok = lambda b, h, d, c: (b * nc + chunk_of(d, c), h)
    in_specs = [pl.BlockSpec((1, 1, 1, 1, 2, CHUNK), lambda b, h, d, c: (b, d, h, chunk_of(d, c), 0, 0)),
                pl.BlockSpec((1, 1, 2, 1), lambda b, h, d, c: (d, h, 0, 0)),
                pl.BlockSpec((CHUNK, dk), tok),
                pl.BlockSpec((CHUNK, dk), tok),
                pl.BlockSpec((CHUNK, dk), tok)]
    args = [gl, gb, q, k, v]
    c_spec = pl.BlockSpec((1, 1, 1, dk, dk), lambda b, h, d, c: (b, d, h, 0, 0))
    n_spec = pl.BlockSpec((1, 1, 1, 1, dk), lambda b, h, d, c: (b, d, h, 0, 0))
    m_spec = pl.BlockSpec((1, 1, 1, 1, 1), lambda b, h, d, c: (b, d, h, 0, 0))
    if has_s0:
        c0, n0, m0 = ctx
        in_specs += [c_spec, n_spec, m_spec]
        args += [c0, n0.reshape(B, 2, H, 1, dk), m0.reshape(B, 2, H, 1, 1)]
    out_specs = [pl.BlockSpec((1, CHUNK, dk), lambda b, h, d, c: (d, b * nc + chunk_of(d, c), h))]
    out_shape = [jax.ShapeDtypeStruct((2, rows, Wd), BF16)]
    if emit_state:
        out_specs += [c_spec, n_spec, m_spec]
        out_shape += [jax.ShapeDtypeStruct((B, 2, H, dk, dk), F32),
                      jax.ShapeDtypeStruct((B, 2, H, 1, dk), F32),
                      jax.ShapeDtypeStruct((B, 2, H, 1, 1), F32)]
    res = pl.pallas_call(
        functools.partial(_mlscan_kernel, nc=nc, dk=dk, has_s0=has_s0, emit_state=emit_state),
        grid=(B, H, 2, nc),
        in_specs=in_specs,
        out_specs=out_specs,
        out_shape=out_shape,
        scratch_shapes=[pltpu.VMEM((dk, dk), F32), pltpu.VMEM((1, dk), F32), pltpu.VMEM((1, 1), F32)],
        compiler_params=_params("parallel", "parallel", "arbitrary", "arbitrary"),
        name="mlstm_scan",
    )(*args)
    hdir = res[0]

    tm = _tile(rows, 256)
    o = pl.pallas_call(
        _mlpost_kernel,
        grid=(rows // tm, H),
        in_specs=[pl.BlockSpec((2, tm, dk), lambda i, h: (0, i, h)),
                  pl.BlockSpec((tm, dk), lambda i, h: (i, h)),
                  pl.BlockSpec((tm, dk), lambda i, h: (i, H + h)),
                  pl.BlockSpec((1, dk), lambda i, h: (0, h)),
                  pl.BlockSpec((1, dk), lambda i, h: (0, h))],
        out_specs=pl.BlockSpec((tm, dk), lambda i, h: (i, h)),
        out_shape=jax.ShapeDtypeStruct((rows, Wd), BF16),
        compiler_params=_params("parallel", "arbitrary"),
        name="mlstm_post",
    )(hdir, xc, u, p['gn_g'].reshape(1, Wd), p['skip'].reshape(1, Wd))
    y = _outproj(o, p['out_w'].astype(BF16), x2, gate, rows_per_mod, *tiles['outproj'])
    new = None
    if emit_state:
        new = (res[1], res[2].reshape(B, 2, H, dk), res[3].reshape(B, 2, H))
    return y, new


_BASE = dict(inproj=(1024, 512), outproj=(1024, 512), tq=256)
TILES_PROMPT = [_BASE, _BASE, _BASE, _BASE]
TILES_SAMPLE = [
    _BASE,
    dict(inproj=(1024, 512), outproj=(512, 2048), tq=256),
    dict(inproj=(512, 1024), outproj=(1024, 512), tq=256),
    dict(inproj=(1024, 1024), outproj=(512, 2048), tq=512),
]

def kernel(x_prompt, x_sample, c, c_ctx, cache_k_l0, cache_v_l0, state_ret_l1, state_C_l2, state_n_l2, state_m_l2, cache_k_l3, cache_v_l3, norm_g_l0, ada_w_l0, ada_b_l0, in_w_l0, out_w_l0, qk_g_l0, lam_l0, subln_g_l0, norm_g_l1, ada_w_l1, ada_b_l1, in_w_l1, out_w_l1, decay_l1, gn_g_l1, norm_g_l2, ada_w_l2, ada_b_l2, in_w_l2, out_w_l2, conv_w_l2, conv_b_l2, qkv_bd_l2, gate_w_l2, gate_b_l2, gn_g_l2, skip_l2, norm_g_l3, ada_w_l3, ada_b_l3, in_w_l3, out_w_l3, qk_g_l3, lam_l3, subln_g_l3):
    layers = [
        dict(norm_g=norm_g_l0, ada_w=ada_w_l0, ada_b=ada_b_l0, in_w=in_w_l0, out_w=out_w_l0,
             qk_g=qk_g_l0, lam=lam_l0, subln_g=subln_g_l0),
        dict(norm_g=norm_g_l1, ada_w=ada_w_l1, ada_b=ada_b_l1, in_w=in_w_l1, out_w=out_w_l1,
             decay=decay_l1, gn_g=gn_g_l1),
        dict(norm_g=norm_g_l2, ada_w=ada_w_l2, ada_b=ada_b_l2, in_w=in_w_l2, out_w=out_w_l2,
             conv_w=conv_w_l2, conv_b=conv_b_l2, qkv_bd=qkv_bd_l2, gate_w=gate_w_l2, gate_b=gate_b_l2,
             gn_g=gn_g_l2, skip=skip_l2),
        dict(norm_g=norm_g_l3, ada_w=ada_w_l3, ada_b=ada_b_l3, in_w=in_w_l3, out_w=out_w_l3,
             qk_g=qk_g_l3, lam=lam_l3, subln_g=subln_g_l3),
    ]
    ctxs = [(cache_k_l0, cache_v_l0), (state_ret_l1,), (state_C_l2, state_n_l2, state_m_l2),
            (cache_k_l3, cache_v_l3)]
    Bp, Lp, D = x_prompt.shape
    Bs, Ls, _ = x_sample.shape
    xp = x_prompt.reshape(Bp * Lp, D)
    xs = x_sample.reshape(Bs * Ls, D)
    n_cond = 1 + Bs
    cond_rows = -(-n_cond // 8) * 8
    cond = jnp.concatenate([c_ctx[None, :], c, jnp.zeros((cond_rows - n_cond, D), F32)], axis=0)
    new_states = []
    for i, p in enumerate(layers):
        m = _adaln(cond, p['ada_w'], p['ada_b'])
        mods_p = tuple(m[0:1, k * D:(k + 1) * D].reshape(1, 1, D) for k in range(3))
        mods_s = tuple(m[1:n_cond, k * D:(k + 1) * D].reshape(Bs, 1, D) for k in range(3))
        kind = i % 3
        tp, ts = TILES_PROMPT[i], TILES_SAMPLE[i]
        if kind == 0:
            xp, st = _diff_attention_layer(xp, Bp, Lp, p, mods_p, Bp * Lp, i, None, tp)
            xs, _ = _diff_attention_layer(xs, Bs, Ls, p, mods_s, Ls, i, ctxs[i], ts)
        elif kind == 1:
            xp, st = _retention_layer(xp, Bp, Lp, p, mods_p, Bp * Lp, None, tp)
            xs, _ = _retention_layer(xs, Bs, Ls, p, mods_s, Ls, ctxs[i], ts)
        else:
            xp, st = _mlstm_layer(xp, Bp, Lp, p, mods_p, Bp * Lp, None, tp)
            xs, _ = _mlstm_layer(xs, Bs, Ls, p, mods_s, Ls, ctxs[i], ts)
        new_states.append(st)
    (k_l0, v_l0), (ret_l1,), (C_l2, n_l2, m_l2), (k_l3, v_l3) = new_states
    return (xp.reshape(Bp, Lp, D), xs.reshape(Bs, Ls, D), k_l0, v_l0, ret_l1, C_l2, n_l2, m_l2, k_l3, v_l3)
```

```python
import functools
import math

import jax
import jax.numpy as jnp
from jax import lax
from jax.experimental import pallas as pl
from jax.experimental.pallas import tpu as pltpu

F32 = jnp.float32
BF16 = jnp.bfloat16

CHUNK = 256
GRID_W = 64
ROPE_BASE = 10000.0
NORM_EPS = 1e-6
LANES = 128
VMEM_LIMIT_BYTES = 56 * 1024 * 1024

LOG2E = math.log2(math.e)
MAX_UNSHIFTED_SCORE = 60.0
KEY_BLOCK = 512

NT_DIMS = (((1,), (1,)), ((), ()))
TN_DIMS = (((0,), (0,)), ((), ()))


def _tile(n, pref):
    if n <= pref:
        return n
    t = pref
    while n % t:
        t -= 1
    return t


def _params(*sem):
    return pltpu.CompilerParams(dimension_semantics=sem, vmem_limit_bytes=VMEM_LIMIT_BYTES)


def _silu(x):
    return x * jax.nn.sigmoid(x)


def _rms(x):
    return x * lax.rsqrt(jnp.mean(x * x, axis=-1, keepdims=True) + NORM_EPS)


def _adaln_kernel(c_ref, w_ref, b_ref, o_ref):
    s = _silu(c_ref[...]).astype(BF16)
    o_ref[...] = jnp.dot(s, w_ref[...].astype(BF16), preferred_element_type=F32) + b_ref[...]


def _adaln(cond, w, b):
    R, D = cond.shape
    N = w.shape[1]
    tn = _tile(N, 512)
    return pl.pallas_call(
        _adaln_kernel,
        grid=(N // tn,),
        in_specs=[pl.BlockSpec((R, D), lambda j: (0, 0)),
                  pl.BlockSpec((D, tn), lambda j: (0, j)),
                  pl.BlockSpec((1, tn), lambda j: (0, j))],
        out_specs=pl.BlockSpec((R, tn), lambda j: (0, j)),
        out_shape=jax.ShapeDtypeStruct((R, N), F32),
        compiler_params=_params("arbitrary"),
        name="adaln",
    )(cond, w, b.reshape(1, N))


def _inproj_kernel(x_ref, g_ref, sc_ref, sh_ref, w_ref, o_ref, h_ref):
    @pl.when(pl.program_id(1) == 0)
    def _():
        gain = g_ref[...] * (1.0 + sc_ref[0])
        h_ref[...] = (_rms(x_ref[...]) * gain + sh_ref[0]).astype(BF16)

    o_ref[...] = jnp.dot(h_ref[...], w_ref[...], preferred_element_type=F32).astype(o_ref.dtype)


def _inproj(x2, norm_g, scale, shift, w, rows_per_mod, out_dtype, tm_pref=1024, tn_pref=512):
    rows, D = x2.shape
    N = w.shape[1]
    tm = _tile(rows_per_mod, tm_pref)
    tn = _tile(N, tn_pref)
    mod = lambda i, j: ((i * tm) // rows_per_mod, 0, 0)
    return pl.pallas_call(
        _inproj_kernel,
        grid=(rows // tm, N // tn),
        in_specs=[pl.BlockSpec((tm, D), lambda i, j: (i, 0)),
                  pl.BlockSpec((1, D), lambda i, j: (0, 0)),
                  pl.BlockSpec((1, 1, D), mod),
                  pl.BlockSpec((1, 1, D), mod),
                  pl.BlockSpec((D, tn), lambda i, j: (0, j))],
        out_specs=pl.BlockSpec((tm, tn), lambda i, j: (i, j)),
        out_shape=jax.ShapeDtypeStruct((rows, N), out_dtype),
        scratch_shapes=[pltpu.VMEM((tm, D), BF16)],
        compiler_params=_params("parallel", "arbitrary"),
        name="inproj",
    )(x2, norm_g.reshape(1, D), scale, shift, w)


def _outproj_kernel(a_ref, w_ref, x_ref, gt_ref, o_ref):
    y = jnp.dot(a_ref[...], w_ref[...], preferred_element_type=F32)
    o_ref[...] = x_ref[...] + gt_ref[0] * y


def _outproj(a, w, x2, gate, rows_per_mod, tm_pref=1024, tn_pref=512):
    rows, K = a.shape
    D = w.shape[1]
    tm = _tile(rows_per_mod, tm_pref)
    tn = _tile(D, tn_pref)
    w_mode = dict(pipeline_mode=pl.Buffered(1)) if tn == D else {}
    return pl.pallas_call(
        _outproj_kernel,
        grid=(rows // tm, D // tn),
        in_specs=[pl.BlockSpec((tm, K), lambda i, j: (i, 0)),
                  pl.BlockSpec((K, tn), lambda i, j: (0, j), **w_mode),
                  pl.BlockSpec((tm, tn), lambda i, j: (i, j)),
                  pl.BlockSpec((1, 1, tn), lambda i, j: ((i * tm) // rows_per_mod, 0, j))],
        out_specs=pl.BlockSpec((tm, tn), lambda i, j: (i, j)),
        out_shape=jax.ShapeDtypeStruct((rows, D), F32),
        compiler_params=_params("parallel", "arbitrary"),
        name="outproj",
    )(a, w, x2, gate)


def _rope_tables(L, dh):
    t = jnp.arange(L)
    row = (t // GRID_W).astype(F32)
    col = (t % GRID_W).astype(F32)
    n_freq = dh // 4
    inv = ROPE_BASE ** (-jnp.arange(n_freq, dtype=F32) / n_freq)
    ar = row[:, None] * inv[None, :]
    ac = col[:, None] * inv[None, :]
    cos = jnp.concatenate([jnp.cos(ar), jnp.cos(ar), jnp.cos(ac), jnp.cos(ac)], axis=-1)
    sin = jnp.concatenate([-jnp.sin(ar), jnp.sin(ar), -jnp.sin(ac), jnp.sin(ac)], axis=-1)
    return cos.astype(F32), sin.astype(F32)


def _qkpost_kernel(*refs, n_sub, dh, rope, emit_k):
    if rope:
        u_ref, g_ref, cos_ref, sin_ref = refs[:4]
        outs = refs[4:]
    else:
        u_ref, g_ref = refs[:2]
        outs = refs[2:]
    o_ref = outs[0]
    g = g_ref[0]
    factor = jnp.where(pl.program_id(1) == 0, dh ** -0.5 * LOG2E, 1.0)
    if rope:
        cos = cos_ref[...]
        sin = sin_ref[...]
        lane = lax.broadcasted_iota(jnp.int32, cos.shape, 1)
        first = (lane % (dh // 2)) < (dh // 4)
    for s in range(n_sub):
        sl = slice(s * dh, (s + 1) * dh)
        y = _rms(u_ref[:, sl].astype(F32)) * g
        if rope:
            swapped = jnp.where(first, pltpu.roll(y, dh - dh // 4, 1), pltpu.roll(y, dh // 4, 1))
            y = y * cos + swapped * sin
        o_ref[:, sl] = (y * factor).astype(BF16)
        if emit_k:
            outs[1][:, sl] = y


def _qkpost(u, qk_g, L, n_sub, dh, rope, emit_k):
    rows = u.shape[0]
    W = n_sub * dh
    tm = _tile(L, 512)
    nl = L // tm
    in_specs = [pl.BlockSpec((tm, W), lambda i, j: (i, j)),
                pl.BlockSpec((1, 1, dh), lambda i, j: (j, 0, 0))]
    args = [u, qk_g.reshape(2, 1, dh)]
    if rope:
        cos, sin = _rope_tables(L, dh)
        in_specs += [pl.BlockSpec((tm, dh), lambda i, j: (i % nl, 0))] * 2
        args += [cos, sin]
    out_specs = [pl.BlockSpec((tm, W), lambda i, j: (i, j))]
    out_shape = [jax.ShapeDtypeStruct((rows, 2 * W), BF16)]
    if emit_k:
        out_specs.append(pl.BlockSpec((tm, W), lambda i, j: (i, 0)))
        out_shape.append(jax.ShapeDtypeStruct((rows, W), F32))
    return pl.pallas_call(
        functools.partial(_qkpost_kernel, n_sub=n_sub, dh=dh, rope=rope, emit_k=emit_k),
        grid=(rows // tm, 2),
        in_specs=in_specs,
        out_specs=out_specs,
        out_shape=out_shape,
        compiler_params=_params("parallel", "arbitrary"),
        name="qkpost",
    )(*args)


def _attn_kernel(*refs, dh, lam_init, has_ctx, stable):
    if has_ctx:
        lam_ref, sg_ref, q_ref, k_ref, v_ref, g_ref, ck_ref, cv_ref, o_ref = refs
    else:
        lam_ref, sg_ref, q_ref, k_ref, v_ref, g_ref, o_ref = refs
    lm = lam_ref[...]
    lam = (jnp.exp(jnp.sum(lm[0:1] * lm[1:2], axis=1, keepdims=True))
           - jnp.exp(jnp.sum(lm[2:3] * lm[3:4], axis=1, keepdims=True)) + lam_init)
    L = k_ref.shape[0]
    o = None
    for c in range(2):
        sl = slice(c * dh, (c + 1) * dh)
        qc = q_ref[:, sl]
        if stable:
            v = v_ref[...].astype(BF16)
            s1 = lax.dot_general(qc, k_ref[:, sl], NT_DIMS, preferred_element_type=F32)
            m = jnp.max(s1, axis=-1, keepdims=True)
            if has_ctx:
                s2 = lax.dot_general(qc, ck_ref[0, :, sl], NT_DIMS, preferred_element_type=F32)
                m = jnp.maximum(m, jnp.max(s2, axis=-1, keepdims=True))
            p1 = jnp.exp2(s1 - m)
            l = jnp.sum(p1, axis=-1, keepdims=True)
            oc = jnp.dot(p1.astype(BF16), v, preferred_element_type=F32)
            if has_ctx:
                p2 = jnp.exp2(s2 - m)
                l = l + jnp.sum(p2, axis=-1, keepdims=True)
                oc = oc + jnp.dot(p2.astype(BF16), cv_ref[0], preferred_element_type=F32)
        else:
            kb = _tile(L, KEY_BLOCK)
            blocks = [(k_ref[s:s + kb, sl], v_ref[s:s + kb, :]) for s in range(0, L, kb)]
            if has_ctx:
                P = ck_ref.shape[1]
                kb = _tile(P, KEY_BLOCK)
                blocks += [(ck_ref[0, s:s + kb, sl], cv_ref[0, s:s + kb, :]) for s in range(0, P, kb)]
            lanes = None
            oc = None
            for kb, vb in blocks:
                p = jnp.exp2(lax.dot_general(qc, kb, NT_DIMS, preferred_element_type=F32))
                for t in range(0, p.shape[1], LANES):
                    lanes = p[:, t:t + LANES] if lanes is None else lanes + p[:, t:t + LANES]
                ob = jnp.dot(p.astype(BF16), vb.astype(BF16), preferred_element_type=F32)
                oc = ob if oc is None else oc + ob
            l = jnp.sum(lanes, axis=-1, keepdims=True)
        coef = (1.0 / l) if c == 0 else (-lam / l)
        o = oc * coef if o is None else o + oc * coef
    o = _rms(o) * sg_ref[...] * (1.0 - lam_init)
    o_ref[...] = (o * _silu(g_ref[...].astype(F32))).astype(BF16)


def _attention(qk, u, lam_p, subln_g, ctx, B, L, H, dh, lam_init, tq_pref, stable):
    rows = qk.shape[0]
    V = 2 * dh
    tq = _tile(L, tq_pref)
    nq = L // tq
    has_ctx = ctx is not None
    in_specs = [pl.BlockSpec((4, dh), lambda b, h, i: (0, 0)),
                pl.BlockSpec((1, V), lambda b, h, i: (0, 0)),
                pl.BlockSpec((tq, V), lambda b, h, i: (b * nq + i, h)),
                pl.BlockSpec((L, V), lambda b, h, i: (b, H + h)),
                pl.BlockSpec((L, V), lambda b, h, i: (b, 2 * H + h)),
                pl.BlockSpec((tq, V), lambda b, h, i: (b * nq + i, 3 * H + h))]
    args = [lam_p, subln_g.reshape(1, V), qk, qk, u, u]
    if has_ctx:
        P = ctx[0].shape[1]
        in_specs += [pl.BlockSpec((1, P, V), lambda b, h, i: (b, 0, h))] * 2
        args += list(ctx)
    return pl.pallas_call(
        functools.partial(_attn_kernel, dh=dh, lam_init=lam_init, has_ctx=has_ctx, stable=stable),
        grid=(B, H, nq),
        in_specs=in_specs,
        out_specs=pl.BlockSpec((tq, V), lambda b, h, i: (b * nq + i, h)),
        out_shape=jax.ShapeDtypeStruct((rows, H * V), BF16),
        compiler_params=_params("parallel", "parallel", "arbitrary"),
        name="diffattn_stable" if stable else "diffattn",
    )(*args)


def _score_bound(qk_g, dh, cache_k):
    qn = math.sqrt(dh) * jnp.max(jnp.abs(qk_g[0])) * (dh ** -0.5 * LOG2E)
    kn = math.sqrt(dh) * jnp.max(jnp.abs(qk_g[1]))
    if cache_k is not None:
        kn = jnp.maximum(kn, jnp.sqrt(jnp.max(jnp.sum(jnp.square(cache_k), axis=-1))))
    return 1.01 * qn * kn


def _diff_attention_layer(x2, B, L, p, mods, rows_per_mod, layer_idx, ctx, tiles):
    shift, scale, gate = mods
    dh = p['qk_g'].shape[1]
    V = p['subln_g'].shape[0]
    QKW = (p['in_w'].shape[1] - 2 * p['out_w'].shape[0]) // 2
    H = p['out_w'].shape[0] // V
    n_sub = QKW // dh
    lam_init = 0.8 - 0.6 * math.exp(-0.3 * layer_idx)
    is_prompt = ctx is None
    u = _inproj(x2, p['norm_g'], scale, shift, p['in_w'].astype(BF16), rows_per_mod,
                F32 if is_prompt else BF16, *tiles['inproj'])
    res = _qkpost(u, p['qk_g'], L, n_sub, dh, rope=not is_prompt, emit_k=is_prompt)
    new = None
    if is_prompt:
        qk, kf = res
        new = (kf.reshape(B, L, n_sub, dh), u[:, 2 * QKW:2 * QKW + H * V].reshape(B, L, H, V))
        cache = None
    else:
        qk = res[0]
        ck, cv = ctx
        P = ck.shape[1]
        cache = (ck.reshape(B, P, QKW).astype(BF16), cv.reshape(B, P, H * V).astype(BF16))
    def attend(stable):
        return _attention(qk, u, p['lam'], p['subln_g'], cache, B, L, H, dh, lam_init, tiles['tq'], stable)

    bound = _score_bound(p['qk_g'], dh, None if is_prompt else ctx[0])
    o = lax.cond(bound <= MAX_UNSHIFTED_SCORE, lambda: attend(False), lambda: attend(True))
    y = _outproj(o, p['out_w'].astype(BF16), x2, gate, rows_per_mod, *tiles['outproj'])
    return y, new


def _ret_kernel(*refs, nc, dk, has_s0, emit_state):
    dec_ref, gn_ref, q_ref, k_ref, v_ref, g_ref = refs[:6]
    i = 6
    if has_s0:
        s0_ref = refs[i]
        i += 1
    o_ref = refs[i]
    i += 1
    if emit_state:
        so_ref = refs[i]
        i += 1
    sf_ref, sb_ref, acc_ref = refs[i:]
    C = CHUNK
    lg = jnp.log1p(-jnp.exp(dec_ref[0]))
    lgf = lg[0:1]
    lgb = lg[1:2]
    ii = lax.broadcasted_iota(jnp.int32, (C, C), 0)
    jj = lax.broadcasted_iota(jnp.int32, (C, C), 1)
    rel = (ii - jj).astype(F32)
    dtot = (jnp.where(rel >= 0, jnp.exp(jnp.maximum(rel, 0.0) * lgf), 0.0)
            + jnp.where(rel <= 0, jnp.exp(jnp.maximum(-rel, 0.0) * lgb), 0.0))
    pos = lax.broadcasted_iota(jnp.int32, (C, 1), 0).astype(F32)
    qdf = jnp.exp((pos + 1.0) * lgf)
    kdf = jnp.exp((C - 1.0 - pos) * lgf)
    qdb = jnp.exp((C - pos) * lgb)
    kdb = jnp.exp(pos * lgb)
    cdf = jnp.exp(C * lgf)
    cdb = jnp.exp(C * lgb)
    scale = dk ** -0.5
    if has_s0:
        sf_ref[...] = s0_ref[0, 0, 0]
        sb_ref[...] = s0_ref[0, 1, 0]
    else:
        sf_ref[...] = jnp.zeros_like(sf_ref)
        sb_ref[...] = jnp.zeros_like(sb_ref)
    gn = gn_ref[0]

    def load(c):
        r = pl.multiple_of(c * C, C)
        q = q_ref[pl.ds(r, C), :]
        k = k_ref[pl.ds(r, C), :].astype(F32) * scale
        v = v_ref[pl.ds(r, C), :].astype(BF16)
        return r, q, k, v

    def fwd(c, carry):
        r, q, k, v = load(c)
        att = lax.dot_general(q, k.astype(BF16), NT_DIMS, preferred_element_type=F32) * dtot
        o = jnp.dot(att.astype(BF16), v, preferred_element_type=F32)
        o = o + jnp.dot((q.astype(F32) * qdf).astype(BF16), sf_ref[...].astype(BF16),
                        preferred_element_type=F32)
        acc_ref[pl.ds(r, C), :] = o
        sf_ref[...] = cdf * sf_ref[...] + lax.dot_general(
            (k * kdf).astype(BF16), v, TN_DIMS, preferred_element_type=F32)
        return carry

    lax.fori_loop(0, nc, fwd, 0, unroll=min(nc, 4))

    def bwd(t, carry):
        r, q, k, v = load(nc - 1 - t)
        o = acc_ref[pl.ds(r, C), :] + jnp.dot(
            (q.astype(F32) * qdb).astype(BF16), sb_ref[...].astype(BF16), preferred_element_type=F32)
        sb_ref[...] = cdb * sb_ref[...] + lax.dot_general(
            (k * kdb).astype(BF16), v, TN_DIMS, preferred_element_type=F32)
        g = g_ref[pl.ds(r, C), :].astype(F32)
        o_ref[pl.ds(r, C), :] = (_rms(o) * gn * _silu(g)).astype(BF16)
        return carry

    lax.fori_loop(0, nc, bwd, 0, unroll=min(nc, 4))
    if emit_state:
        so_ref[0, 0, 0] = sf_ref[...]
        so_ref[0, 1, 0] = sb_ref[...]


def _retention_layer(x2, B, L, p, mods, rows_per_mod, ctx, tiles):
    shift, scale, gate = mods
    H, dv = p['gn_g'].shape
    dk = (p['in_w'].shape[1] - 2 * H * dv) // (2 * H)
    nc = L // CHUNK
    rows = x2.shape[0]
    has_s0 = ctx is not None
    emit_state = not has_s0
    u = _inproj(x2, p['norm_g'], scale, shift, p['in_w'].astype(BF16), rows_per_mod, BF16,
                *tiles['inproj'])
    kq = (H * dk) // dk
    kv = (2 * H * dk) // dv
    in_specs = [pl.BlockSpec((1, 2, 1), lambda b, h: (h, 0, 0)),
                pl.BlockSpec((1, 1, dv), lambda b, h: (h, 0, 0)),
                pl.BlockSpec((L, dk), lambda b, h: (b, h)),
                pl.BlockSpec((L, dk), lambda b, h: (b, kq + h)),
                pl.BlockSpec((L, dv), lambda b, h: (b, kv + h)),
                pl.BlockSpec((L, dv), lambda b, h: (b, kv + H + h))]
    args = [p['decay'].T.reshape(H, 2, 1), p['gn_g'].reshape(H, 1, dv), u, u, u, u]
    st_spec = pl.BlockSpec((1, 2, 1, dk, dv), lambda b, h: (b, 0, h, 0, 0))
    if has_s0:
        in_specs.append(st_spec)
        args.append(ctx[0])
    out_specs = [pl.BlockSpec((L, dv), lambda b, h: (b, h))]
    out_shape = [jax.ShapeDtypeStruct((rows, H * dv), BF16)]
    if emit_state:
        out_specs.append(st_spec)
        out_shape.append(jax.ShapeDtypeStruct((B, 2, H, dk, dv), F32))
    res = pl.pallas_call(
        functools.partial(_ret_kernel, nc=nc, dk=dk, has_s0=has_s0, emit_state=emit_state),
        grid=(B, H),
        in_specs=in_specs,
        out_specs=out_specs,
        out_shape=out_shape,
        scratch_shapes=[pltpu.VMEM((dk, dv), F32), pltpu.VMEM((dk, dv), F32), pltpu.VMEM((L, dv), F32)],
        compiler_params=_params("parallel", "parallel"),
        name="retention",
    )(*args)
    y = _outproj(res[0], p['out_w'].astype(BF16), x2, gate, rows_per_mod, *tiles['outproj'])
    return y, ((res[1],) if emit_state else None)


def _mlpre_kernel(xm_ref, cw_ref, cb_ref, wbd_ref, gw_ref, xc_ref, q_ref, k_ref, v_ref, gt_ref, *, bw):
    L, tc = xm_ref.shape
    x = xm_ref[...].astype(F32)
    cw = cw_ref[...]
    W = cw.shape[0]
    pad = (W - 1) // 2
    row = lax.broadcasted_iota(jnp.int32, (L, tc), 0)
    acc = x * cw[pad:pad + 1] + cb_ref[...]
    for w in range(W):
        d = w - pad
        if d == 0:
            continue
        sh = pltpu.roll(x, (-d) % L, 0)
        valid = (row < L - d) if d > 0 else (row >= -d)
        acc = acc + jnp.where(valid, sh, 0.0) * cw[w:w + 1]
    xc = _silu(acc)
    xc_ref[...] = xc.astype(BF16)
    xcb = xc.astype(BF16)
    xmb = xm_ref[...].astype(BF16)
    gsum = jnp.zeros(gt_ref.shape, F32)
    for blk in range(tc // bw):
        sl = slice(blk * bw, (blk + 1) * bw)
        q = jnp.dot(xcb[:, sl], wbd_ref[0, blk], preferred_element_type=F32).astype(BF16)
        k = jnp.dot(xcb[:, sl], wbd_ref[1, blk], preferred_element_type=F32).astype(BF16)
        v = jnp.dot(xmb[:, sl], wbd_ref[2, blk], preferred_element_type=F32).astype(BF16)
        q_ref[:, sl] = q
        k_ref[:, sl] = k
        v_ref[:, sl] = v
        gsum = gsum + jnp.dot(q, gw_ref[0, sl, :], preferred_element_type=F32)
        gsum = gsum + jnp.dot(k, gw_ref[1, sl, :], preferred_element_type=F32)
        gsum = gsum + jnp.dot(v, gw_ref[2, sl, :], preferred_element_type=F32)

    @pl.when(pl.program_id(1) == 0)
    def _():
        gt_ref[...] = gsum

    @pl.when(pl.program_id(1) != 0)
    def _():
        gt_ref[...] = gt_ref[...] + gsum


def _logsigmoid(x):
    return jnp.minimum(x, 0.0) - jnp.log1p(jnp.exp(-jnp.abs(x)))


def _mlscan_kernel(*refs, nc, dk, has_s0, emit_state):
    gl_ref, gb_ref, q_ref, k_ref, v_ref = refs[:5]
    i = 5
    if has_s0:
        c0_ref, n0_ref, m0_ref = refs[i:i + 3]
        i += 3
    h_ref = refs[i]
    i += 1
    if emit_state:
        co_ref, no_ref, mo_ref = refs[i:i + 3]
        i += 3
    c_ref, n_ref, m_ref = refs[i:]
    C = CHUNK
    d = pl.program_id(2)
    c = pl.program_id(3)

    @pl.when(c == 0)
    def _():
        if has_s0:
            c_ref[...] = c0_ref[0, 0, 0]
            n_ref[...] = n0_ref[0, 0, 0]
            m_ref[...] = m0_ref[0, 0, 0]
        else:
            c_ref[...] = jnp.zeros_like(c_ref)
            n_ref[...] = jnp.zeros_like(n_ref)
            m_ref[...] = jnp.zeros_like(m_ref)

    gl = gl_ref[0, 0, 0, 0] + gb_ref[0, 0]
    ih_row = gl[0:1]
    f_row = _logsigmoid(gl[1:2])
    ii = lax.broadcasted_iota(jnp.int32, (C, C), 0)
    jj = lax.broadcasted_iota(jnp.int32, (C, C), 1)
    sgn = 1 - 2 * d
    seen = (ii - jj) * sgn >= 0
    seen_t = (jj - ii) * sgn >= 0
    eye = ii == jj
    fmat = jnp.broadcast_to(f_row, (C, C))
    imat = jnp.broadcast_to(ih_row, (C, C))
    bh_col = jnp.sum(jnp.where(seen, fmat, 0.0), axis=1, keepdims=True)
    f_col = jnp.sum(jnp.where(eye, fmat, 0.0), axis=1, keepdims=True)
    ih_col = jnp.sum(jnp.where(eye, imat, 0.0), axis=1, keepdims=True)
    bh_row = jnp.sum(jnp.where(seen_t, jnp.broadcast_to(f_col, (C, C)), 0.0), axis=0, keepdims=True)
    tot = jnp.sum(f_row, axis=1, keepdims=True)
    m_prev = m_ref[...]
    dlog = jnp.where(seen, bh_col - bh_row + ih_row, -jnp.inf)
    inter = bh_col + m_prev
    m_i = jnp.maximum(jnp.max(dlog, axis=1, keepdims=True), inter)
    w = jnp.exp(dlog - m_i)
    wi = jnp.exp(inter - m_i)
    q = q_ref[...]
    kf = k_ref[...].astype(F32) * (dk ** -0.5)
    v = v_ref[...]
    s = lax.dot_general(q, kf.astype(BF16), NT_DIMS, preferred_element_type=F32) * w
    num = jnp.dot(s.astype(BF16), v, preferred_element_type=F32)
    num = num + wi * jnp.dot(q, c_ref[...].astype(BF16), preferred_element_type=F32)
    qn = jnp.sum(q.astype(F32) * n_ref[...], axis=1, keepdims=True)
    den = jnp.sum(s, axis=1, keepdims=True) + wi * qn
    norm = jnp.maximum(jnp.abs(den), jnp.exp(-m_i))
    h_ref[0] = (num / norm).astype(h_ref.dtype)
    kw = tot - bh_col + ih_col
    m_new = jnp.maximum(tot + m_prev, jnp.max(kw, axis=0, keepdims=True))
    wk = jnp.exp(kw - m_new)
    dec = jnp.exp(tot + m_prev - m_new)
    kcw = kf * wk
    c_ref[...] = dec * c_ref[...] + lax.dot_general(kcw.astype(BF16), v, TN_DIMS,
                                                   preferred_element_type=F32)
    n_ref[...] = dec * n_ref[...] + jnp.sum(kcw, axis=0, keepdims=True)
    m_ref[...] = m_new

    if emit_state:
        @pl.when(c == nc - 1)
        def _():
            co_ref[0, 0, 0] = c_ref[...]
            no_ref[0, 0, 0] = n_ref[...]
            mo_ref[0, 0, 0] = m_ref[...]


def _mlpost_kernel(h_ref, xc_ref, z_ref, gn_ref, sk_ref, o_ref):
    o = _rms(h_ref[0].astype(F32) + h_ref[1].astype(F32)) * gn_ref[...]
    o = (o + sk_ref[...] * xc_ref[...].astype(F32)) * _silu(z_ref[...].astype(F32))
    o_ref[...] = o.astype(BF16)


def _mlstm_layer(x2, B, L, p, mods, rows_per_mod, ctx, tiles):
    shift, scale, gate = mods
    H, dk = p['gn_g'].shape
    Wd = H * dk
    bsz = p['qkv_bd'].shape[-1]
    nc = L // CHUNK
    rows = x2.shape[0]
    has_s0 = ctx is not None
    emit_state = not has_s0
    u = _inproj(x2, p['norm_g'], scale, shift, p['in_w'].astype(BF16), rows_per_mod, BF16,
                *tiles['inproj'])

    bw = 256 if Wd % 256 == 0 else LANES
    nb = Wd // bw
    per = bw // bsz
    bd = p['qkv_bd'].reshape(3, nb, per, bsz, bsz)
    eye = jnp.eye(per, dtype=F32)
    wbd = jnp.einsum('tngio,gh->tngiho', bd, eye).reshape(3, nb, bw, bw).astype(BF16)
    n_gate = p['gate_w'].shape[-1]
    gw = p['gate_w'].reshape(2, 3, Wd, n_gate).transpose(1, 2, 0, 3).reshape(3, Wd, 2 * n_gate)
    gw = jnp.pad(gw, ((0, 0), (0, 0), (0, LANES - 2 * n_gate))).astype(BF16)

    tc = bw
    nbt = tc // bw
    xc, q, k, v, gates = pl.pallas_call(
        functools.partial(_mlpre_kernel, bw=bw),
        grid=(B, Wd // tc),
        in_specs=[pl.BlockSpec((L, tc), lambda b, j: (b, j)),
                  pl.BlockSpec((p['conv_w'].shape[0], tc), lambda b, j: (0, j)),
                  pl.BlockSpec((1, tc), lambda b, j: (0, j)),
                  pl.BlockSpec((3, nbt, bw, bw), lambda b, j: (0, j, 0, 0)),
                  pl.BlockSpec((3, tc, LANES), lambda b, j: (0, j, 0))],
        out_specs=[pl.BlockSpec((L, tc), lambda b, j: (b, j))] * 4
                  + [pl.BlockSpec((L, LANES), lambda b, j: (b, 0))],
        out_shape=[jax.ShapeDtypeStruct((rows, Wd), BF16)] * 4
                  + [jax.ShapeDtypeStruct((rows, LANES), F32)],
        compiler_params=_params("parallel", "arbitrary"),
        name="mlstm_pre",
    )(u, p['conv_w'], p['conv_b'].reshape(1, Wd), wbd, gw)

    gl = gates[:, :2 * n_gate].reshape(B, nc, CHUNK, 2, 2, H).transpose(0, 3, 5, 1, 4, 2)
    gb = p['gate_b'].reshape(2, 2, H).transpose(0, 2, 1).reshape(2, H, 2, 1)

    def chunk_of(d, c):
        return c + d * (nc - 1 - 2 * c)

    tok = lambda b, h, d, c: (b * nc + chunk_of(d, c), h)
    in_specs = [pl.BlockSpec((1, 1, 1, 1, 2, CHUNK), lambda b, h, d, c: (b, d, h, chunk_of(d, c), 0, 0)),
                pl.BlockSpec((1, 1, 2, 1), lambda b, h, d, c: (d, h, 0, 0)),
                pl.BlockSpec((CHUNK, dk), tok),
                pl.BlockSpec((CHUNK, dk), tok),
                pl.BlockSpec((CHUNK, dk), tok)]
    args = [gl, gb, q, k, v]
    c_spec = pl.BlockSpec((1, 1, 1, dk, dk), lambda b, h, d, c: (b, d, h, 0, 0))
    n_spec = pl.BlockSpec((1, 1, 1, 1, dk), lambda b, h, d, c: (b, d, h, 0, 0))
    m_spec = pl.BlockSpec((1, 1, 1, 1, 1), lambda b, h, d, c: (b, d, h, 0, 0))
    if has_s0:
        c0, n0, m0 = ctx
        in_specs += [c_spec, n_spec, m_spec]
        args += [c0, n0.reshape(B, 2, H, 1, dk), m0.reshape(B, 2, H, 1, 1)]
    out_specs = [pl.BlockSpec((1, CHUNK, dk), lambda b, h, d, c: (d, b * nc + chunk_of(d, c), h))]
    out_shape = [jax.ShapeDtypeStruct((2, rows, Wd), BF16)]
    if emit_state:
        out_specs += [c_spec, n_spec, m_spec]
        out_shape += [jax.ShapeDtypeStruct((B, 2, H, dk, dk), F32),
                      jax.ShapeDtypeStruct((B, 2, H, 1, dk), F32),
                      jax.ShapeDtypeStruct((B, 2, H, 1, 1), F32)]
    res = pl.pallas_call(
        functools.partial(_mlscan_kernel, nc=nc, dk=dk, has_s0=has_s0, emit_state=emit_state),
        grid=(B, H, 2, nc),
        in_specs=in_specs,
        out_specs=out_specs,
        out_shape=out_shape,
        scratch_shapes=[pltpu.VMEM((dk, dk), F32), pltpu.VMEM((1, dk), F32), pltpu.VMEM((1, 1), F32)],
        compiler_params=_params("parallel", "parallel", "arbitrary", "arbitrary"),
        name="mlstm_scan",
    )(*args)
    hdir = res[0]

    tm = _tile(rows, 256)
    o = pl.pallas_call(
        _mlpost_kernel,
        grid=(rows // tm, H),
        in_specs=[pl.BlockSpec((2, tm, dk), lambda i, h: (0, i, h)),
                  pl.BlockSpec((tm, dk), lambda i, h: (i, h)),
                  pl.BlockSpec((tm, dk), lambda i, h: (i, H + h)),
                  pl.BlockSpec((1, dk), lambda i, h: (0, h)),
                  pl.BlockSpec((1, dk), lambda i, h: (0, h))],
        out_specs=pl.BlockSpec((tm, dk), lambda i, h: (i, h)),
        out_shape=jax.ShapeDtypeStruct((rows, Wd), BF16),
        compiler_params=_params("parallel", "arbitrary"),
        name="mlstm_post",
    )(hdir, xc, u, p['gn_g'].reshape(1, Wd), p['skip'].reshape(1, Wd))
    y = _outproj(o, p['out_w'].astype(BF16), x2, gate, rows_per_mod, *tiles['outproj'])
    new = None
    if emit_state:
        new = (res[1], res[2].reshape(B, 2, H, dk), res[3].reshape(B, 2, H))
    return y, new


_BASE = dict(inproj=(1024, 1024), outproj=(512, 2048), tq=512)
TILES_PROMPT = [_BASE, _BASE, _BASE, _BASE]
TILES_SAMPLE = [_BASE, _BASE, _BASE, dict(_BASE, tq=1024)]

def kernel(x_prompt, x_sample, c, c_ctx, cache_k_l0, cache_v_l0, state_ret_l1, state_C_l2, state_n_l2, state_m_l2, cache_k_l3, cache_v_l3, norm_g_l0, ada_w_l0, ada_b_l0, in_w_l0, out_w_l0, qk_g_l0, lam_l0, subln_g_l0, norm_g_l1, ada_w_l1, ada_b_l1, in_w_l1, out_w_l1, decay_l1, gn_g_l1, norm_g_l2, ada_w_l2, ada_b_l2, in_w_l2, out_w_l2, conv_w_l2, conv_b_l2, qkv_bd_l2, gate_w_l2, gate_b_l2, gn_g_l2, skip_l2, norm_g_l3, ada_w_l3, ada_b_l3, in_w_l3, out_w_l3, qk_g_l3, lam_l3, subln_g_l3):
    layers = [
        dict(norm_g=norm_g_l0, ada_w=ada_w_l0, ada_b=ada_b_l0, in_w=in_w_l0, out_w=out_w_l0,
             qk_g=qk_g_l0, lam=lam_l0, subln_g=subln_g_l0),
        dict(norm_g=norm_g_l1, ada_w=ada_w_l1, ada_b=ada_b_l1, in_w=in_w_l1, out_w=out_w_l1,
             decay=decay_l1, gn_g=gn_g_l1),
        dict(norm_g=norm_g_l2, ada_w=ada_w_l2, ada_b=ada_b_l2, in_w=in_w_l2, out_w=out_w_l2,
             conv_w=conv_w_l2, conv_b=conv_b_l2, qkv_bd=qkv_bd_l2, gate_w=gate_w_l2, gate_b=gate_b_l2,
             gn_g=gn_g_l2, skip=skip_l2),
        dict(norm_g=norm_g_l3, ada_w=ada_w_l3, ada_b=ada_b_l3, in_w=in_w_l3, out_w=out_w_l3,
             qk_g=qk_g_l3, lam=lam_l3, subln_g=subln_g_l3),
    ]
    ctxs = [(cache_k_l0, cache_v_l0), (state_ret_l1,), (state_C_l2, state_n_l2, state_m_l2),
            (cache_k_l3, cache_v_l3)]
    Bp, Lp, D = x_prompt.shape
    Bs, Ls, _ = x_sample.shape
    xp = x_prompt.reshape(Bp * Lp, D)
    xs = x_sample.reshape(Bs * Ls, D)
    n_cond = 1 + Bs
    cond_rows = -(-n_cond // 8) * 8
    cond = jnp.concatenate([c_ctx[None, :], c, jnp.zeros((cond_rows - n_cond, D), F32)], axis=0)
    new_states = []
    for i, p in enumerate(layers):
        m = _adaln(cond, p['ada_w'], p['ada_b'])
        mods_p = tuple(m[0:1, k * D:(k + 1) * D].reshape(1, 1, D) for k in range(3))
        mods_s = tuple(m[1:n_cond, k * D:(k + 1) * D].reshape(Bs, 1, D) for k in range(3))
        kind = i % 3
        tp, ts = TILES_PROMPT[i], TILES_SAMPLE[i]
        if kind == 0:
            xp, st = _diff_attention_layer(xp, Bp, Lp, p, mods_p, Bp * Lp, i, None, tp)
            xs, _ = _diff_attention_layer(xs, Bs, Ls, p, mods_s, Ls, i, ctxs[i], ts)
        elif kind == 1:
            xp, st = _retention_layer(xp, Bp, Lp, p, mods_p, Bp * Lp, None, tp)
            xs, _ = _retention_layer(xs, Bs, Ls, p, mods_s, Ls, ctxs[i], ts)
        else:
            xp, st = _mlstm_layer(xp, Bp, Lp, p, mods_p, Bp * Lp, None, tp)
            xs, _ = _mlstm_layer(xs, Bs, Ls, p, mods_s, Ls, ctxs[i], ts)
        new_states.append(st)
    (k_l0, v_l0), (ret_l1,), (C_l2, n_l2, m_l2), (k_l3, v_l3) = new_states
    return (xp.reshape(Bp, Lp, D), xs.reshape(Bs, Ls, D), k_l0, v_l0, ret_l1, C_l2, n_l2, m_l2, k_l3, v_l3)
```

```python
import functools
import math

import jax
import jax.numpy as jnp
from jax import lax
from jax.experimental import pallas as pl
from jax.experimental.pallas import tpu as pltpu

F32 = jnp.float32
BF16 = jnp.bfloat16

CHUNK = 256
GRID_W = 64
ROPE_BASE = 10000.0
NORM_EPS = 1e-6
LANES = 128
VMEM_LIMIT_BYTES = 56 * 1024 * 1024

LOG2E = math.log2(math.e)
MAX_UNSHIFTED_SCORE = 60.0
KEY_BLOCK = 512
STATE_SLAB = 512
SCAN_CHUNKS_PER_STEP = 4

NT_DIMS = (((1,), (1,)), ((), ()))
TN_DIMS = (((0,), (0,)), ((), ()))


def _tile(n, pref):
    if n <= pref:
        return n
    t = pref
    while n % t:
        t -= 1
    return t


def _params(*sem):
    return pltpu.CompilerParams(dimension_semantics=sem, vmem_limit_bytes=VMEM_LIMIT_BYTES)


def _silu(x):
    return x * jax.nn.sigmoid(x)


def _rms(x):
    return x * lax.rsqrt(jnp.mean(x * x, axis=-1, keepdims=True) + NORM_EPS)


def _adaln_kernel(c_ref, w_ref, b_ref, o_ref):
    s = _silu(c_ref[...]).astype(BF16)
    o_ref[...] = jnp.dot(s, w_ref[...].astype(BF16), preferred_element_type=F32) + b_ref[...]


def _adaln(cond, w, b):
    R, D = cond.shape
    N = w.shape[1]
    tn = _tile(N, 512)
    return pl.pallas_call(
        _adaln_kernel,
        grid=(N // tn,),
        in_specs=[pl.BlockSpec((R, D), lambda j: (0, 0)),
                  pl.BlockSpec((D, tn), lambda j: (0, j)),
                  pl.BlockSpec((1, tn), lambda j: (0, j))],
        out_specs=pl.BlockSpec((R, tn), lambda j: (0, j)),
        out_shape=jax.ShapeDtypeStruct((R, N), F32),
        compiler_params=_params("arbitrary"),
        name="adaln",
    )(cond, w, b.reshape(1, N))


def _inproj_kernel(x_ref, g_ref, sc_ref, sh_ref, w_ref, o_ref, h_ref):
    @pl.when(pl.program_id(1) == 0)
    def _():
        gain = g_ref[...] * (1.0 + sc_ref[0])
        h_ref[...] = (_rms(x_ref[...]) * gain + sh_ref[0]).astype(BF16)

    o_ref[...] = jnp.dot(h_ref[...], w_ref[...], preferred_element_type=F32).astype(o_ref.dtype)


def _inproj(x2, norm_g, scale, shift, w, rows_per_mod, out_dtype, tm_pref=1024, tn_pref=512):
    rows, D = x2.shape
    N = w.shape[1]
    tm = _tile(rows_per_mod, tm_pref)
    tn = _tile(N, tn_pref)
    mod = lambda i, j: ((i * tm) // rows_per_mod, 0, 0)
    return pl.pallas_call(
        _inproj_kernel,
        grid=(rows // tm, N // tn),
        in_specs=[pl.BlockSpec((tm, D), lambda i, j: (i, 0)),
                  pl.BlockSpec((1, D), lambda i, j: (0, 0)),
                  pl.BlockSpec((1, 1, D), mod),
                  pl.BlockSpec((1, 1, D), mod),
                  pl.BlockSpec((D, tn), lambda i, j: (0, j))],
        out_specs=pl.BlockSpec((tm, tn), lambda i, j: (i, j)),
        out_shape=jax.ShapeDtypeStruct((rows, N), out_dtype),
        scratch_shapes=[pltpu.VMEM((tm, D), BF16)],
        compiler_params=_params("parallel", "arbitrary"),
        name="inproj",
    )(x2, norm_g.reshape(1, D), scale, shift, w)


def _outproj_kernel(a_ref, w_ref, x_ref, gt_ref, o_ref):
    y = jnp.dot(a_ref[...], w_ref[...], preferred_element_type=F32)
    o_ref[...] = x_ref[...] + gt_ref[0] * y


def _outproj(a, w, x2, gate, rows_per_mod, tm_pref=1024, tn_pref=512):
    rows, K = a.shape
    D = w.shape[1]
    tm = _tile(rows_per_mod, tm_pref)
    tn = _tile(D, tn_pref)
    w_mode = dict(pipeline_mode=pl.Buffered(1)) if tn == D else {}
    return pl.pallas_call(
        _outproj_kernel,
        grid=(rows // tm, D // tn),
        in_specs=[pl.BlockSpec((tm, K), lambda i, j: (i, 0)),
                  pl.BlockSpec((K, tn), lambda i, j: (0, j), **w_mode),
                  pl.BlockSpec((tm, tn), lambda i, j: (i, j)),
                  pl.BlockSpec((1, 1, tn), lambda i, j: ((i * tm) // rows_per_mod, 0, j))],
        out_specs=pl.BlockSpec((tm, tn), lambda i, j: (i, j)),
        out_shape=jax.ShapeDtypeStruct((rows, D), F32),
        compiler_params=_params("parallel", "arbitrary"),
        name="outproj",
    )(a, w, x2, gate)


def _rope_tables(L, dh):
    t = jnp.arange(L)
    row = (t // GRID_W).astype(F32)
    col = (t % GRID_W).astype(F32)
    n_freq = dh // 4
    inv = ROPE_BASE ** (-jnp.arange(n_freq, dtype=F32) / n_freq)
    ar = row[:, None] * inv[None, :]
    ac = col[:, None] * inv[None, :]
    cos = jnp.concatenate([jnp.cos(ar), jnp.cos(ar), jnp.cos(ac), jnp.cos(ac)], axis=-1)
    sin = jnp.concatenate([-jnp.sin(ar), jnp.sin(ar), -jnp.sin(ac), jnp.sin(ac)], axis=-1)
    return cos.astype(F32), sin.astype(F32)


def _rope_gain_tables(L, dh, qk_g):
    cos, sin = _rope_tables(L, dh)
    scale = jnp.array([dh ** -0.5 * LOG2E, 1.0], F32)[:, None]
    g = qk_g.astype(F32) * scale
    g_swapped = g.reshape(2, 2, 2, dh // 4)[:, :, ::-1, :].reshape(2, dh)
    return g[:, None, :] * cos[None], g_swapped[:, None, :] * sin[None]


def _qkpost_kernel(*refs, n_sub, dh, rope, emit_k):
    if rope:
        u_ref, t1_ref, t2_ref, o_ref = refs
        t1 = t1_ref[0]
        t2 = t2_ref[0]
        lane = lax.broadcasted_iota(jnp.int32, t1.shape, 1)
        first = (lane % (dh // 2)) < (dh // 4)
        for s in range(n_sub):
            sl = slice(s * dh, (s + 1) * dh)
            x = u_ref[:, sl].astype(F32)
            r = lax.rsqrt(jnp.mean(x * x, axis=-1, keepdims=True) + NORM_EPS)
            swapped = jnp.where(first, pltpu.roll(x, dh - dh // 4, 1), pltpu.roll(x, dh // 4, 1))
            o_ref[:, sl] = ((x * t1 + swapped * t2) * r).astype(BF16)
        return
    u_ref, g_ref = refs[:2]
    outs = refs[2:]
    o_ref = outs[0]
    g = g_ref[0]
    factor = jnp.where(pl.program_id(1) == 0, dh ** -0.5 * LOG2E, 1.0)
    for s in range(n_sub):
        sl = slice(s * dh, (s + 1) * dh)
        y = _rms(u_ref[:, sl].astype(F32)) * g
        o_ref[:, sl] = (y * factor).astype(BF16)
        if emit_k:
            outs[1][:, sl] = y


def _qkpost(u, qk_g, L, n_sub, dh, rope, emit_k):
    rows = u.shape[0]
    W = n_sub * dh
    tm = _tile(L, 512)
    nl = L // tm
    in_specs = [pl.BlockSpec((tm, W), lambda i, j: (i, j))]
    if rope:
        in_specs += [pl.BlockSpec((1, tm, dh), lambda i, j: (j, i % nl, 0))] * 2
        args = [u, *_rope_gain_tables(L, dh, qk_g)]
    else:
        in_specs.append(pl.BlockSpec((1, 1, dh), lambda i, j: (j, 0, 0)))
        args = [u, qk_g.reshape(2, 1, dh)]
    out_specs = [pl.BlockSpec((tm, W), lambda i, j: (i, j))]
    out_shape = [jax.ShapeDtypeStruct((rows, 2 * W), BF16)]
    if emit_k:
        out_specs.append(pl.BlockSpec((tm, W), lambda i, j: (i, 0)))
        out_shape.append(jax.ShapeDtypeStruct((rows, W), F32))
    return pl.pallas_call(
        functools.partial(_qkpost_kernel, n_sub=n_sub, dh=dh, rope=rope, emit_k=emit_k),
        grid=(rows // tm, 2),
        in_specs=in_specs,
        out_specs=out_specs,
        out_shape=out_shape,
        compiler_params=_params("parallel", "arbitrary"),
        name="qkpost",
    )(*args)


def _attn_kernel(*refs, dh, lam_init, has_ctx, stable):
    if has_ctx:
        lam_ref, sg_ref, q_ref, k_ref, v_ref, g_ref, ck_ref, cv_ref, o_ref = refs
    else:
        lam_ref, sg_ref, q_ref, k_ref, v_ref, g_ref, o_ref = refs
    lm = lam_ref[...]
    lam = (jnp.exp(jnp.sum(lm[0:1] * lm[1:2], axis=1, keepdims=True))
           - jnp.exp(jnp.sum(lm[2:3] * lm[3:4], axis=1, keepdims=True)) + lam_init)
    L = k_ref.shape[0]
    o = None
    for c in range(2):
        sl = slice(c * dh, (c + 1) * dh)
        qc = q_ref[:, sl]
        if stable:
            v = v_ref[...].astype(BF16)
            s1 = lax.dot_general(qc, k_ref[:, sl], NT_DIMS, preferred_element_type=F32)
            m = jnp.max(s1, axis=-1, keepdims=True)
            if has_ctx:
                s2 = lax.dot_general(qc, ck_ref[0, :, sl], NT_DIMS, preferred_element_type=F32)
                m = jnp.maximum(m, jnp.max(s2, axis=-1, keepdims=True))
            p1 = jnp.exp2(s1 - m)
            l = jnp.sum(p1, axis=-1, keepdims=True)
            oc = jnp.dot(p1.astype(BF16), v, preferred_element_type=F32)
            if has_ctx:
                p2 = jnp.exp2(s2 - m)
                l = l + jnp.sum(p2, axis=-1, keepdims=True)
                oc = oc + jnp.dot(p2.astype(BF16), cv_ref[0], preferred_element_type=F32)
        else:
            kb = _tile(L, KEY_BLOCK)
            blocks = [(k_ref[s:s + kb, sl], v_ref[s:s + kb, :]) for s in range(0, L, kb)]
            if has_ctx:
                P = ck_ref.shape[1]
                kb = _tile(P, KEY_BLOCK)
                blocks += [(ck_ref[0, s:s + kb, sl], cv_ref[0, s:s + kb, :]) for s in range(0, P, kb)]
            lanes = None
            oc = None
            for kb, vb in blocks:
                p = jnp.exp2(lax.dot_general(qc, kb, NT_DIMS, preferred_element_type=F32))
                for t in range(0, p.shape[1], LANES):
                    lanes = p[:, t:t + LANES] if lanes is None else lanes + p[:, t:t + LANES]
                ob = jnp.dot(p.astype(BF16), vb.astype(BF16), preferred_element_type=F32)
                oc = ob if oc is None else oc + ob
            l = jnp.sum(lanes, axis=-1, keepdims=True)
        coef = (1.0 / l) if c == 0 else (-lam / l)
        o = oc * coef if o is None else o + oc * coef
    o = _rms(o) * sg_ref[...] * (1.0 - lam_init)
    o_ref[...] = (o * _silu(g_ref[...].astype(F32))).astype(BF16)


def _attention(qk, u, lam_p, subln_g, ctx, B, L, H, dh, lam_init, tq_pref, stable):
    rows = qk.shape[0]
    V = 2 * dh
    tq = _tile(L, tq_pref)
    nq = L // tq
    has_ctx = ctx is not None
    in_specs = [pl.BlockSpec((4, dh), lambda b, h, i: (0, 0)),
                pl.BlockSpec((1, V), lambda b, h, i: (0, 0)),
                pl.BlockSpec((tq, V), lambda b, h, i: (b * nq + i, h)),
                pl.BlockSpec((L, V), lambda b, h, i: (b, H + h)),
                pl.BlockSpec((L, V), lambda b, h, i: (b, 2 * H + h)),
                pl.BlockSpec((tq, V), lambda b, h, i: (b * nq + i, 3 * H + h))]
    args = [lam_p, subln_g.reshape(1, V), qk, qk, u, u]
    if has_ctx:
        P = ctx[0].shape[1]
        in_specs += [pl.BlockSpec((1, P, V), lambda b, h, i: (b, 0, h))] * 2
        args += list(ctx)
    return pl.pallas_call(
        functools.partial(_attn_kernel, dh=dh, lam_init=lam_init, has_ctx=has_ctx, stable=stable),
        grid=(B, H, nq),
        in_specs=in_specs,
        out_specs=pl.BlockSpec((tq, V), lambda b, h, i: (b * nq + i, h)),
        out_shape=jax.ShapeDtypeStruct((rows, H * V), BF16),
        compiler_params=_params("parallel", "parallel", "arbitrary"),
        name="diffattn_stable" if stable else "diffattn",
    )(*args)


def _score_bound(qk_g, dh, cache_k):
    qn = math.sqrt(dh) * jnp.max(jnp.abs(qk_g[0])) * (dh ** -0.5 * LOG2E)
    kn = math.sqrt(dh) * jnp.max(jnp.abs(qk_g[1]))
    if cache_k is not None:
        kn = jnp.maximum(kn, jnp.sqrt(jnp.max(jnp.sum(jnp.square(cache_k), axis=-1))))
    return 1.01 * qn * kn


def _diff_attention_layer(x2, B, L, p, mods, rows_per_mod, layer_idx, ctx, tiles):
    shift, scale, gate = mods
    dh = p['qk_g'].shape[1]
    V = p['subln_g'].shape[0]
    QKW = (p['in_w'].shape[1] - 2 * p['out_w'].shape[0]) // 2
    H = p['out_w'].shape[0] // V
    n_sub = QKW // dh
    lam_init = 0.8 - 0.6 * math.exp(-0.3 * layer_idx)
    is_prompt = ctx is None
    u = _inproj(x2, p['norm_g'], scale, shift, p['in_w'].astype(BF16), rows_per_mod,
                F32 if is_prompt else BF16, *tiles['inproj'])
    res = _qkpost(u, p['qk_g'], L, n_sub, dh, rope=not is_prompt, emit_k=is_prompt)
    new = None
    if is_prompt:
        qk, kf = res
        new = (kf.reshape(B, L, n_sub, dh), u[:, 2 * QKW:2 * QKW + H * V].reshape(B, L, H, V))
        cache = None
    else:
        qk = res[0]
        ck, cv = ctx
        P = ck.shape[1]
        cache = (ck.reshape(B, P, QKW).astype(BF16), cv.reshape(B, P, H * V).astype(BF16))
    def attend(stable):
        return _attention(qk, u, p['lam'], p['subln_g'], cache, B, L, H, dh, lam_init, tiles['tq'], stable)

    bound = _score_bound(p['qk_g'], dh, None if is_prompt else ctx[0])
    o = lax.cond(bound <= MAX_UNSHIFTED_SCORE, lambda: attend(False), lambda: attend(True))
    y = _outproj(o, p['out_w'].astype(BF16), x2, gate, rows_per_mod, *tiles['outproj'])
    return y, new


def _ret_kernel(*refs, nc, dk, has_s0, emit_state):
    dec_ref, gn_ref, q_ref, k_ref, v_ref, g_ref = refs[:6]
    i = 6
    if has_s0:
        s0_ref = refs[i]
        i += 1
    o_ref = refs[i]
    i += 1
    if emit_state:
        so_ref = refs[i]
        i += 1
    sf_ref, sb_ref, acc_ref = refs[i:]
    C = CHUNK
    lg = jnp.log1p(-jnp.exp(dec_ref[0]))
    lgf = lg[0:1]
    lgb = lg[1:2]
    ii = lax.broadcasted_iota(jnp.int32, (C, C), 0)
    jj = lax.broadcasted_iota(jnp.int32, (C, C), 1)
    rel = (ii - jj).astype(F32)
    dtot = (jnp.where(rel >= 0, jnp.exp(jnp.maximum(rel, 0.0) * lgf), 0.0)
            + jnp.where(rel <= 0, jnp.exp(jnp.maximum(-rel, 0.0) * lgb), 0.0))
    pos = lax.broadcasted_iota(jnp.int32, (C, 1), 0).astype(F32)
    qdf = jnp.exp((pos + 1.0) * lgf)
    kdf = jnp.exp((C - 1.0 - pos) * lgf)
    qdb = jnp.exp((C - pos) * lgb)
    kdb = jnp.exp(pos * lgb)
    cdf = jnp.exp(C * lgf)
    cdb = jnp.exp(C * lgb)
    scale = dk ** -0.5
    if has_s0:
        sf_ref[...] = s0_ref[0, 0, 0]
        sb_ref[...] = s0_ref[0, 1, 0]
    else:
        sf_ref[...] = jnp.zeros_like(sf_ref)
        sb_ref[...] = jnp.zeros_like(sb_ref)
    gn = gn_ref[0]

    def load(c):
        r = pl.multiple_of(c * C, C)
        q = q_ref[pl.ds(r, C), :]
        k = k_ref[pl.ds(r, C), :].astype(F32) * scale
        v = v_ref[pl.ds(r, C), :].astype(BF16)
        return r, q, k, v

    def fwd(c, carry):
        r, q, k, v = load(c)
        att = lax.dot_general(q, k.astype(BF16), NT_DIMS, preferred_element_type=F32) * dtot
        o = jnp.dot(att.astype(BF16), v, preferred_element_type=F32)
        o = o + jnp.dot((q.astype(F32) * qdf).astype(BF16), sf_ref[...].astype(BF16),
                        preferred_element_type=F32)
        acc_ref[pl.ds(r, C), :] = o
        sf_ref[...] = cdf * sf_ref[...] + lax.dot_general(
            (k * kdf).astype(BF16), v, TN_DIMS, preferred_element_type=F32)
        return carry

    lax.fori_loop(0, nc, fwd, 0, unroll=min(nc, 4))

    def bwd(t, carry):
        r, q, k, v = load(nc - 1 - t)
        o = acc_ref[pl.ds(r, C), :] + jnp.dot(
            (q.astype(F32) * qdb).astype(BF16), sb_ref[...].astype(BF16), preferred_element_type=F32)
        sb_ref[...] = cdb * sb_ref[...] + lax.dot_general(
            (k * kdb).astype(BF16), v, TN_DIMS, preferred_element_type=F32)
        g = g_ref[pl.ds(r, C), :].astype(F32)
        o_ref[pl.ds(r, C), :] = (_rms(o) * gn * _silu(g)).astype(BF16)
        return carry

    lax.fori_loop(0, nc, bwd, 0, unroll=min(nc, 4))
    if emit_state:
        so_ref[0, 0, 0] = sf_ref[...]
        so_ref[0, 1, 0] = sb_ref[...]


def _retention_layer(x2, B, L, p, mods, rows_per_mod, ctx, tiles):
    shift, scale, gate = mods
    H, dv = p['gn_g'].shape
    dk = (p['in_w'].shape[1] - 2 * H * dv) // (2 * H)
    nc = L // CHUNK
    rows = x2.shape[0]
    has_s0 = ctx is not None
    emit_state = not has_s0
    u = _inproj(x2, p['norm_g'], scale, shift, p['in_w'].astype(BF16), rows_per_mod, BF16,
                *tiles['inproj'])
    kq = (H * dk) // dk
    kv = (2 * H * dk) // dv
    in_specs = [pl.BlockSpec((1, 2, 1), lambda b, h: (h, 0, 0)),
                pl.BlockSpec((1, 1, dv), lambda b, h: (h, 0, 0)),
                pl.BlockSpec((L, dk), lambda b, h: (b, h)),
                pl.BlockSpec((L, dk), lambda b, h: (b, kq + h)),
                pl.BlockSpec((L, dv), lambda b, h: (b, kv + h)),
                pl.BlockSpec((L, dv), lambda b, h: (b, kv + H + h))]
    args = [p['decay'].T.reshape(H, 2, 1), p['gn_g'].reshape(H, 1, dv), u, u, u, u]
    st_spec = pl.BlockSpec((1, 2, 1, dk, dv), lambda b, h: (b, 0, h, 0, 0))
    if has_s0:
        in_specs.append(st_spec)
        args.append(ctx[0])
    out_specs = [pl.BlockSpec((L, dv), lambda b, h: (b, h))]
    out_shape = [jax.ShapeDtypeStruct((rows, H * dv), BF16)]
    if emit_state:
        out_specs.append(st_spec)
        out_shape.append(jax.ShapeDtypeStruct((B, 2, H, dk, dv), F32))
    res = pl.pallas_call(
        functools.partial(_ret_kernel, nc=nc, dk=dk, has_s0=has_s0, emit_state=emit_state),
        grid=(B, H),
        in_specs=in_specs,
        out_specs=out_specs,
        out_shape=out_shape,
        scratch_shapes=[pltpu.VMEM((dk, dv), F32), pltpu.VMEM((dk, dv), F32), pltpu.VMEM((L, dv), F32)],
        compiler_params=_params("parallel", "parallel"),
        name="retention",
    )(*args)
    y = _outproj(res[0], p['out_w'].astype(BF16), x2, gate, rows_per_mod, *tiles['outproj'])
    return y, ((res[1],) if emit_state else None)


def _mlpre_kernel(xm_ref, cw_ref, cb_ref, wbd_ref, gw_ref, xc_ref, q_ref, k_ref, v_ref, gt_ref, *, bw):
    L, tc = xm_ref.shape
    x = xm_ref[...].astype(F32)
    cw = cw_ref[...]
    W = cw.shape[0]
    pad = (W - 1) // 2
    row = lax.broadcasted_iota(jnp.int32, (L, tc), 0)
    acc = x * cw[pad:pad + 1] + cb_ref[...]
    for w in range(W):
        d = w - pad
        if d == 0:
            continue
        sh = pltpu.roll(x, (-d) % L, 0)
        valid = (row < L - d) if d > 0 else (row >= -d)
        acc = acc + jnp.where(valid, sh, 0.0) * cw[w:w + 1]
    xc = _silu(acc)
    xc_ref[...] = xc.astype(BF16)
    xcb = xc.astype(BF16)
    xmb = xm_ref[...].astype(BF16)
    gsum = jnp.zeros(gt_ref.shape, F32)
    for blk in range(tc // bw):
        sl = slice(blk * bw, (blk + 1) * bw)
        q = jnp.dot(xcb[:, sl], wbd_ref[0, blk], preferred_element_type=F32).astype(BF16)
        k = jnp.dot(xcb[:, sl], wbd_ref[1, blk], preferred_element_type=F32).astype(BF16)
        v = jnp.dot(xmb[:, sl], wbd_ref[2, blk], preferred_element_type=F32).astype(BF16)
        q_ref[:, sl] = q
        k_ref[:, sl] = k
        v_ref[:, sl] = v
        gsum = gsum + jnp.dot(q, gw_ref[0, sl, :], preferred_element_type=F32)
        gsum = gsum + jnp.dot(k, gw_ref[1, sl, :], preferred_element_type=F32)
        gsum = gsum + jnp.dot(v, gw_ref[2, sl, :], preferred_element_type=F32)

    @pl.when(pl.program_id(1) == 0)
    def _():
        gt_ref[...] = gsum

    @pl.when(pl.program_id(1) != 0)
    def _():
        gt_ref[...] = gt_ref[...] + gsum


def _logsigmoid(x):
    return jnp.minimum(x, 0.0) - jnp.log1p(jnp.exp(-jnp.abs(x)))


def _mlscan_chunk(gl_ref, gb_ref, q_ref, k_ref, v_ref, h_ref, c_ref, n_ref, m_ref, d, dk):
    C = CHUNK
    gl = gl_ref[0, 0, 0, 0] + gb_ref[0, 0]
    ih_row = gl[0:1]
    f_row = _logsigmoid(gl[1:2])
    ii = lax.broadcasted_iota(jnp.int32, (C, C), 0)
    jj = lax.broadcasted_iota(jnp.int32, (C, C), 1)
    sgn = 1 - 2 * d
    seen = (ii - jj) * sgn >= 0
    seen_t = (jj - ii) * sgn >= 0
    eye = ii == jj
    fmat = jnp.broadcast_to(f_row, (C, C))
    imat = jnp.broadcast_to(ih_row, (C, C))
    bh_col = jnp.sum(jnp.where(seen, fmat, 0.0), axis=1, keepdims=True)
    f_col = jnp.sum(jnp.where(eye, fmat, 0.0), axis=1, keepdims=True)
    ih_col = jnp.sum(jnp.where(eye, imat, 0.0), axis=1, keepdims=True)
    bh_row = jnp.sum(jnp.where(seen_t, jnp.broadcast_to(f_col, (C, C)), 0.0), axis=0, keepdims=True)
    tot = jnp.sum(f_row, axis=1, keepdims=True)
    m_prev = m_ref[...]
    dlog = jnp.where(seen, bh_col - bh_row + ih_row, -jnp.inf)
    inter = bh_col + m_prev
    m_i = jnp.maximum(jnp.max(dlog, axis=1, keepdims=True), inter)
    w = jnp.exp(dlog - m_i)
    wi = jnp.exp(inter - m_i)
    q = q_ref[...]
    kscale = dk ** -0.5
    s = lax.dot_general(q, k_ref[...], NT_DIMS, preferred_element_type=F32) * (w * kscale)
    sb = s.astype(BF16)
    qn = jnp.sum(q.astype(F32) * n_ref[...], axis=1, keepdims=True)
    den = jnp.sum(s, axis=1, keepdims=True) + wi * qn
    inv_norm = 1.0 / jnp.maximum(jnp.abs(den), jnp.exp(-m_i))
    kw = tot - bh_col + ih_col
    m_new = jnp.maximum(tot + m_prev, jnp.max(kw, axis=0, keepdims=True))
    wk = jnp.exp(kw - m_new)
    dec = jnp.exp(tot + m_prev - m_new)
    kcw = k_ref[...].astype(F32) * (wk * kscale)
    kcb = kcw.astype(BF16)
    slab = _tile(dk, STATE_SLAB)
    for t in range(0, dk, slab):
        cols = slice(t, t + slab)
        vt = v_ref[:, cols]
        num = jnp.dot(sb, vt, preferred_element_type=F32)
        num = num + wi * jnp.dot(q, c_ref[:, cols].astype(BF16), preferred_element_type=F32)
        h_ref[0, :, cols] = (num * inv_norm).astype(h_ref.dtype)
        c_ref[:, cols] = dec * c_ref[:, cols] + lax.dot_general(kcb, vt, TN_DIMS,
                                                               preferred_element_type=F32)
    n_ref[...] = dec * n_ref[...] + jnp.sum(kcw, axis=0, keepdims=True)
    m_ref[...] = m_new


def _mlscan_kernel(*refs, nblk, cps, dk, has_s0, emit_state):
    gl_ref, gb_ref, q_ref, k_ref, v_ref = refs[:5]
    i = 5
    if has_s0:
        c0_ref, n0_ref, m0_ref = refs[i:i + 3]
        i += 3
    h_ref = refs[i]
    i += 1
    if emit_state:
        co_ref, no_ref, mo_ref = refs[i:i + 3]
        i += 3
    c_ref, n_ref, m_ref = refs[i:]
    d = pl.program_id(2)
    c = pl.program_id(3)

    @pl.when(c == 0)
    def _():
        if has_s0:
            c_ref[...] = c0_ref[0, 0, 0]
            n_ref[...] = n0_ref[0, 0, 0]
            m_ref[...] = m0_ref[0, 0, 0]
        else:
            c_ref[...] = jnp.zeros_like(c_ref)
            n_ref[...] = jnp.zeros_like(n_ref)
            m_ref[...] = jnp.zeros_like(m_ref)

    for step in range(cps):
        j = step + d * (cps - 1 - 2 * step)
        rows = pl.ds(pl.multiple_of(j * CHUNK, CHUNK), CHUNK)
        _mlscan_chunk(gl_ref.at[:, :, :, pl.ds(j, 1)], gb_ref, q_ref.at[rows, :], k_ref.at[rows, :],
                      v_ref.at[rows, :], h_ref.at[:, rows, :], c_ref, n_ref, m_ref, d, dk)

    if emit_state:
        @pl.when(c == nblk - 1)
        def _():
            co_ref[0, 0, 0] = c_ref[...]
            no_ref[0, 0, 0] = n_ref[...]
            mo_ref[0, 0, 0] = m_ref[...]


def _mlpost_kernel(h_ref, xc_ref, z_ref, gn_ref, sk_ref, o_ref):
    o = _rms(h_ref[0].astype(F32) + h_ref[1].astype(F32)) * gn_ref[...]
    o = (o + sk_ref[...] * xc_ref[...].astype(F32)) * _silu(z_ref[...].astype(F32))
    o_ref[...] = o.astype(BF16)


def _mlstm_layer(x2, B, L, p, mods, rows_per_mod, ctx, tiles):
    shift, scale, gate = mods
    H, dk = p['gn_g'].shape
    Wd = H * dk
    bsz = p['qkv_bd'].shape[-1]
    nc = L // CHUNK
    rows = x2.shape[0]
    has_s0 = ctx is not None
    emit_state = not has_s0
    u = _inproj(x2, p['norm_g'], scale, shift, p['in_w'].astype(BF16), rows_per_mod, BF16,
                *tiles['inproj'])

    bw = 256 if Wd % 256 == 0 else LANES
    nb = Wd // bw
    per = bw // bsz
    bd = p['qkv_bd'].reshape(3, nb, per, bsz, bsz)
    eye = jnp.eye(per, dtype=F32)
    wbd = jnp.einsum('tngio,gh->tngiho', bd, eye).reshape(3, nb, bw, bw).astype(BF16)
    n_gate = p['gate_w'].shape[-1]
    gw = p['gate_w'].reshape(2, 3, Wd, n_gate).transpose(1, 2, 0, 3).reshape(3, Wd, 2 * n_gate)
    gw = jnp.pad(gw, ((0, 0), (0, 0), (0, LANES - 2 * n_gate))).astype(BF16)

    tc = _tile(Wd, 512)
    nbt = tc // bw
    xc, q, k, v, gates = pl.pallas_call(
        functools.partial(_mlpre_kernel, bw=bw),
        grid=(B, Wd // tc),
        in_specs=[pl.BlockSpec((L, tc), lambda b, j: (b, j)),
                  pl.BlockSpec((p['conv_w'].shape[0], tc), lambda b, j: (0, j)),
                  pl.BlockSpec((1, tc), lambda b, j: (0, j)),
                  pl.BlockSpec((3, nbt, bw, bw), lambda b, j: (0, j, 0, 0)),
                  pl.BlockSpec((3, tc, LANES), lambda b, j: (0, j, 0))],
        out_specs=[pl.BlockSpec((L, tc), lambda b, j: (b, j))] * 4
                  + [pl.BlockSpec((L, LANES), lambda b, j: (b, 0))],
        out_shape=[jax.ShapeDtypeStruct((rows, Wd), BF16)] * 4
                  + [jax.ShapeDtypeStruct((rows, LANES), F32)],
        compiler_params=_params("parallel", "arbitrary"),
        name="mlstm_pre",
    )(u, p['conv_w'], p['conv_b'].reshape(1, Wd), wbd, gw)

    gl = gates[:, :2 * n_gate].reshape(B, nc, CHUNK, 2, 2, H).transpose(0, 3, 5, 1, 4, 2)
    gb = p['gate_b'].reshape(2, 2, H).transpose(0, 2, 1).reshape(2, H, 2, 1)

    cps = _tile(nc, SCAN_CHUNKS_PER_STEP)
    nblk = nc // cps
    blk = cps * CHUNK

    def block_of(d, c):
        return c + d * (nblk - 1 - 2 * c)

    tok = lambda b, h, d, c: (b * nblk + block_of(d, c), h)
    in_specs = [pl.BlockSpec((1, 1, 1, cps, 2, CHUNK), lambda b, h, d, c: (b, d, h, block_of(d, c), 0, 0)),
                pl.BlockSpec((1, 1, 2, 1), lambda b, h, d, c: (d, h, 0, 0)),
                pl.BlockSpec((blk, dk), tok),
                pl.BlockSpec((blk, dk), tok),
                pl.BlockSpec((blk, dk), tok)]
    args = [gl, gb, q, k, v]
    c_spec = pl.BlockSpec((1, 1, 1, dk, dk), lambda b, h, d, c: (b, d, h, 0, 0))
    n_spec = pl.BlockSpec((1, 1, 1, 1, dk), lambda b, h, d, c: (b, d, h, 0, 0))
    m_spec = pl.BlockSpec((1, 1, 1, 1, 1), lambda b, h, d, c: (b, d, h, 0, 0))
    if has_s0:
        c0, n0, m0 = ctx
        in_specs += [c_spec, n_spec, m_spec]
        args += [c0, n0.reshape(B, 2, H, 1, dk), m0.reshape(B, 2, H, 1, 1)]
    out_specs = [pl.BlockSpec((1, blk, dk), lambda b, h, d, c: (d, b * nblk + block_of(d, c), h))]
    out_shape = [jax.ShapeDtypeStruct((2, rows, Wd), BF16)]
    if emit_state:
        out_specs += [c_spec, n_spec, m_spec]
        out_shape += [jax.ShapeDtypeStruct((B, 2, H, dk, dk), F32),
                      jax.ShapeDtypeStruct((B, 2, H, 1, dk), F32),
                      jax.ShapeDtypeStruct((B, 2, H, 1, 1), F32)]
    res = pl.pallas_call(
        functools.partial(_mlscan_kernel, nblk=nblk, cps=cps, dk=dk, has_s0=has_s0, emit_state=emit_state),
        grid=(B, H, 2, nblk),
        in_specs=in_specs,
        out_specs=out_specs,
        out_shape=out_shape,
        scratch_shapes=[pltpu.VMEM((dk, dk), F32), pltpu.VMEM((1, dk), F32), pltpu.VMEM((1, 1), F32)],
        compiler_params=_params("parallel", "parallel", "arbitrary", "arbitrary"),
        name="mlstm_scan",
    )(*args)
    hdir = res[0]

    tm = _tile(rows, 1024)
    o = pl.pallas_call(
        _mlpost_kernel,
        grid=(rows // tm, H),
        in_specs=[pl.BlockSpec((2, tm, dk), lambda i, h: (0, i, h)),
                  pl.BlockSpec((tm, dk), lambda i, h: (i, h)),
                  pl.BlockSpec((tm, dk), lambda i, h: (i, H + h)),
                  pl.BlockSpec((1, dk), lambda i, h: (0, h)),
                  pl.BlockSpec((1, dk), lambda i, h: (0, h))],
        out_specs=pl.BlockSpec((tm, dk), lambda i, h: (i, h)),
        out_shape=jax.ShapeDtypeStruct((rows, Wd), BF16),
        compiler_params=_params("parallel", "arbitrary"),
        name="mlstm_post",
    )(hdir, xc, u, p['gn_g'].reshape(1, Wd), p['skip'].reshape(1, Wd))
    y = _outproj(o, p['out_w'].astype(BF16), x2, gate, rows_per_mod, *tiles['outproj'])
    new = None
    if emit_state:
        new = (res[1], res[2].reshape(B, 2, H, dk), res[3].reshape(B, 2, H))
    return y, new


_BASE = dict(inproj=(1024, 2048), outproj=(512, 2048), tq=512)
_F32_OUT = dict(_BASE, inproj=(1024, 1024))
TILES_PROMPT = [_F32_OUT, _BASE, _BASE, _F32_OUT]
TILES_SAMPLE = [_BASE, _BASE, _BASE, dict(_BASE, tq=1024)]

def kernel(x_prompt, x_sample, c, c_ctx, cache_k_l0, cache_v_l0, state_ret_l1, state_C_l2, state_n_l2, state_m_l2, cache_k_l3, cache_v_l3, norm_g_l0, ada_w_l0, ada_b_l0, in_w_l0, out_w_l0, qk_g_l0, lam_l0, subln_g_l0, norm_g_l1, ada_w_l1, ada_b_l1, in_w_l1, out_w_l1, decay_l1, gn_g_l1, norm_g_l2, ada_w_l2, ada_b_l2, in_w_l2, out_w_l2, conv_w_l2, conv_b_l2, qkv_bd_l2, gate_w_l2, gate_b_l2, gn_g_l2, skip_l2, norm_g_l3, ada_w_l3, ada_b_l3, in_w_l3, out_w_l3, qk_g_l3, lam_l3, subln_g_l3):
    layers = [
        dict(norm_g=norm_g_l0, ada_w=ada_w_l0, ada_b=ada_b_l0, in_w=in_w_l0, out_w=out_w_l0,
             qk_g=qk_g_l0, lam=lam_l0, subln_g=subln_g_l0),
        dict(norm_g=norm_g_l1, ada_w=ada_w_l1, ada_b=ada_b_l1, in_w=in_w_l1, out_w=out_w_l1,
             decay=decay_l1, gn_g=gn_g_l1),
        dict(norm_g=norm_g_l2, ada_w=ada_w_l2, ada_b=ada_b_l2, in_w=in_w_l2, out_w=out_w_l2,
             conv_w=conv_w_l2, conv_b=conv_b_l2, qkv_bd=qkv_bd_l2, gate_w=gate_w_l2, gate_b=gate_b_l2,
             gn_g=gn_g_l2, skip=skip_l2),
        dict(norm_g=norm_g_l3, ada_w=ada_w_l3, ada_b=ada_b_l3, in_w=in_w_l3, out_w=out_w_l3,
             qk_g=qk_g_l3, lam=lam_l3, subln_g=subln_g_l3),
    ]
    ctxs = [(cache_k_l0, cache_v_l0), (state_ret_l1,), (state_C_l2, state_n_l2, state_m_l2),
            (cache_k_l3, cache_v_l3)]
    Bp, Lp, D = x_prompt.shape
    Bs, Ls, _ = x_sample.shape
    xp = x_prompt.reshape(Bp * Lp, D)
    xs = x_sample.reshape(Bs * Ls, D)
    n_cond = 1 + Bs
    cond_rows = -(-n_cond // 8) * 8
    cond = jnp.concatenate([c_ctx[None, :], c, jnp.zeros((cond_rows - n_cond, D), F32)], axis=0)
    new_states = []
    for i, p in enumerate(layers):
        m = _adaln(cond, p['ada_w'], p['ada_b'])
        mods_p = tuple(m[0:1, k * D:(k + 1) * D].reshape(1, 1, D) for k in range(3))
        mods_s = tuple(m[1:n_cond, k * D:(k + 1) * D].reshape(Bs, 1, D) for k in range(3))
        kind = i % 3
        tp, ts = TILES_PROMPT[i], TILES_SAMPLE[i]
        if kind == 0:
            xp, st = _diff_attention_layer(xp, Bp, Lp, p, mods_p, Bp * Lp, i, None, tp)
            xs, _ = _diff_attention_layer(xs, Bs, Ls, p, mods_s, Ls, i, ctxs[i], ts)
        elif kind == 1:
            xp, st = _retention_layer(xp, Bp, Lp, p, mods_p, Bp * Lp, None, tp)
            xs, _ = _retention_layer(xs, Bs, Ls, p, mods_s, Ls, ctxs[i], ts)
        else:
            xp, st = _mlstm_layer(xp, Bp, Lp, p, mods_p, Bp * Lp, None, tp)
            xs, _ = _mlstm_layer(xs, Bs, Ls, p, mods_s, Ls, ctxs[i], ts)
        new_states.append(st)
    (k_l0, v_l0), (ret_l1,), (C_l2, n_l2, m_l2), (k_l3, v_l3) = new_states
    return (xp.reshape(Bp, Lp, D), xs.reshape(Bs, Ls, D), k_l0, v_l0, ret_l1, C_l2, n_l2, m_l2, k_l3, v_l3)
```

```python
import functools
import math

import jax
import jax.numpy as jnp
from jax import lax
from jax.experimental import pallas as pl
from jax.experimental.pallas import tpu as pltpu

F32 = jnp.float32
BF16 = jnp.bfloat16

CHUNK = 256
GRID_W = 64
ROPE_BASE = 10000.0
NORM_EPS = 1e-6
LANES = 128
VMEM_LIMIT_BYTES = 56 * 1024 * 1024

LOG2E = math.log2(math.e)
MAX_UNSHIFTED_SCORE = 60.0
KEY_BLOCK = 512
ROPE_SLAB = 256
STATE_SLAB = 512
SCAN_CHUNKS_PER_STEP = 4

NT_DIMS = (((1,), (1,)), ((), ()))
TN_DIMS = (((0,), (0,)), ((), ()))


def _tile(n, pref):
    if n <= pref:
        return n
    t = pref
    while n % t:
        t -= 1
    return t


def _params(*sem):
    return pltpu.CompilerParams(dimension_semantics=sem, vmem_limit_bytes=VMEM_LIMIT_BYTES)


def _silu(x):
    return x * jax.nn.sigmoid(x)


def _rms(x):
    return x * lax.rsqrt(jnp.mean(x * x, axis=-1, keepdims=True) + NORM_EPS)


def _adaln_kernel(c_ref, w_ref, b_ref, o_ref):
    s = _silu(c_ref[...]).astype(BF16)
    o_ref[...] = jnp.dot(s, w_ref[...].astype(BF16), preferred_element_type=F32) + b_ref[...]


def _adaln(cond, w, b):
    R, D = cond.shape
    N = w.shape[1]
    tn = _tile(N, 512)
    return pl.pallas_call(
        _adaln_kernel,
        grid=(N // tn,),
        in_specs=[pl.BlockSpec((R, D), lambda j: (0, 0)),
                  pl.BlockSpec((D, tn), lambda j: (0, j)),
                  pl.BlockSpec((1, tn), lambda j: (0, j))],
        out_specs=pl.BlockSpec((R, tn), lambda j: (0, j)),
        out_shape=jax.ShapeDtypeStruct((R, N), F32),
        compiler_params=_params("arbitrary"),
        name="adaln",
    )(cond, w, b.reshape(1, N))


def _inproj_kernel(x_ref, g_ref, sc_ref, sh_ref, w_ref, o_ref, h_ref):
    @pl.when(pl.program_id(1) == 0)
    def _():
        gain = g_ref[...] * (1.0 + sc_ref[0])
        h_ref[...] = (_rms(x_ref[...]) * gain + sh_ref[0]).astype(BF16)

    o_ref[...] = jnp.dot(h_ref[...], w_ref[...], preferred_element_type=F32).astype(o_ref.dtype)


def _inproj(x2, norm_g, scale, shift, w, rows_per_mod, out_dtype, tm_pref=1024, tn_pref=512):
    rows, D = x2.shape
    N = w.shape[1]
    tm = _tile(rows_per_mod, tm_pref)
    tn = _tile(N, tn_pref)
    mod = lambda i, j: ((i * tm) // rows_per_mod, 0, 0)
    return pl.pallas_call(
        _inproj_kernel,
        grid=(rows // tm, N // tn),
        in_specs=[pl.BlockSpec((tm, D), lambda i, j: (i, 0)),
                  pl.BlockSpec((1, D), lambda i, j: (0, 0)),
                  pl.BlockSpec((1, 1, D), mod),
                  pl.BlockSpec((1, 1, D), mod),
                  pl.BlockSpec((D, tn), lambda i, j: (0, j))],
        out_specs=pl.BlockSpec((tm, tn), lambda i, j: (i, j)),
        out_shape=jax.ShapeDtypeStruct((rows, N), out_dtype),
        scratch_shapes=[pltpu.VMEM((tm, D), BF16)],
        compiler_params=_params("parallel", "arbitrary"),
        name="inproj",
    )(x2, norm_g.reshape(1, D), scale, shift, w)


def _inproj_rope_kernel(x_ref, g_ref, sc_ref, sh_ref, w_ref, t1_ref, t2_ref, o_ref, h_ref, *, n_qk_tiles, dh):
    j = pl.program_id(1)

    @pl.when(j == 0)
    def _():
        gain = g_ref[...] * (1.0 + sc_ref[0])
        h_ref[...] = (_rms(x_ref[...]) * gain + sh_ref[0]).astype(BF16)

    tm, tn = o_ref.shape

    @pl.when(j < n_qk_tiles)
    def _():
        lane = lax.broadcasted_iota(jnp.int32, (ROPE_SLAB, dh), 1)
        first = (lane % (dh // 2)) < (dh // 4)
        for r0 in range(0, tm, ROPE_SLAB):
            rows = slice(r0, r0 + ROPE_SLAB)
            acc = jnp.dot(h_ref[rows, :], w_ref[...], preferred_element_type=F32)
            t1 = t1_ref[0, rows, :]
            t2 = t2_ref[0, rows, :]
            for s in range(tn // dh):
                sl = slice(s * dh, (s + 1) * dh)
                x = acc[:, sl]
                r = lax.rsqrt(jnp.mean(x * x, axis=-1, keepdims=True) + NORM_EPS)
                swapped = jnp.where(first, pltpu.roll(x, dh - dh // 4, 1), pltpu.roll(x, dh // 4, 1))
                o_ref[rows, sl] = ((x * t1 + swapped * t2) * r).astype(o_ref.dtype)

    @pl.when(j >= n_qk_tiles)
    def _():
        o_ref[...] = jnp.dot(h_ref[...], w_ref[...], preferred_element_type=F32).astype(o_ref.dtype)


def _inproj_rope(x2, norm_g, scale, shift, w, L, qk_g, dh, qk_width, tm_pref, tn_pref):
    rows, D = x2.shape
    N = w.shape[1]
    tm = _tile(L, tm_pref)
    tn = _tile(qk_width, tn_pref)
    nl = L // tm
    n_qk_tiles = 2 * qk_width // tn
    t1, t2 = _rope_gain_tables(L, dh, qk_g)
    mod = lambda i, j: ((i * tm) // L, 0, 0)
    tab = lambda i, j: (jnp.minimum(j // (n_qk_tiles // 2), 1), i % nl, 0)
    return pl.pallas_call(
        functools.partial(_inproj_rope_kernel, n_qk_tiles=n_qk_tiles, dh=dh),
        grid=(rows // tm, N // tn),
        in_specs=[pl.BlockSpec((tm, D), lambda i, j: (i, 0)),
                  pl.BlockSpec((1, D), lambda i, j: (0, 0)),
                  pl.BlockSpec((1, 1, D), mod),
                  pl.BlockSpec((1, 1, D), mod),
                  pl.BlockSpec((D, tn), lambda i, j: (0, j)),
                  pl.BlockSpec((1, tm, dh), tab),
                  pl.BlockSpec((1, tm, dh), tab)],
        out_specs=pl.BlockSpec((tm, tn), lambda i, j: (i, j)),
        out_shape=jax.ShapeDtypeStruct((rows, N), BF16),
        scratch_shapes=[pltpu.VMEM((tm, D), BF16)],
        compiler_params=_params("parallel", "arbitrary"),
        name="inproj_rope",
    )(x2, norm_g.reshape(1, D), scale, shift, w, t1, t2)


def _outproj_kernel(a_ref, w_ref, x_ref, gt_ref, o_ref):
    y = jnp.dot(a_ref[...], w_ref[...], preferred_element_type=F32)
    o_ref[...] = x_ref[...] + gt_ref[0] * y


def _outproj(a, w, x2, gate, rows_per_mod, tm_pref=1024, tn_pref=512):
    rows, K = a.shape
    D = w.shape[1]
    tm = _tile(rows_per_mod, tm_pref)
    tn = _tile(D, tn_pref)
    w_mode = dict(pipeline_mode=pl.Buffered(1)) if tn == D else {}
    return pl.pallas_call(
        _outproj_kernel,
        grid=(rows // tm, D // tn),
        in_specs=[pl.BlockSpec((tm, K), lambda i, j: (i, 0)),
                  pl.BlockSpec((K, tn), lambda i, j: (0, j), **w_mode),
                  pl.BlockSpec((tm, tn), lambda i, j: (i, j)),
                  pl.BlockSpec((1, 1, tn), lambda i, j: ((i * tm) // rows_per_mod, 0, j))],
        out_specs=pl.BlockSpec((tm, tn), lambda i, j: (i, j)),
        out_shape=jax.ShapeDtypeStruct((rows, D), F32),
        compiler_params=_params("parallel", "arbitrary"),
        name="outproj",
    )(a, w, x2, gate)


def _rope_tables(L, dh):
    t = jnp.arange(L)
    row = (t // GRID_W).astype(F32)
    col = (t % GRID_W).astype(F32)
    n_freq = dh // 4
    inv = ROPE_BASE ** (-jnp.arange(n_freq, dtype=F32) / n_freq)
    ar = row[:, None] * inv[None, :]
    ac = col[:, None] * inv[None, :]
    cos = jnp.concatenate([jnp.cos(ar), jnp.cos(ar), jnp.cos(ac), jnp.cos(ac)], axis=-1)
    sin = jnp.concatenate([-jnp.sin(ar), jnp.sin(ar), -jnp.sin(ac), jnp.sin(ac)], axis=-1)
    return cos.astype(F32), sin.astype(F32)


def _rope_gain_tables(L, dh, qk_g):
    cos, sin = _rope_tables(L, dh)
    scale = jnp.array([dh ** -0.5 * LOG2E, 1.0], F32)[:, None]
    g = qk_g.astype(F32) * scale
    g_swapped = g.reshape(2, 2, 2, dh // 4)[:, :, ::-1, :].reshape(2, dh)
    return g[:, None, :] * cos[None], g_swapped[:, None, :] * sin[None]


def _qkpost_kernel(u_ref, g_ref, o_ref, kf_ref, *, n_sub, dh):
    g = g_ref[0]
    factor = jnp.where(pl.program_id(1) == 0, dh ** -0.5 * LOG2E, 1.0)
    for s in range(n_sub):
        sl = slice(s * dh, (s + 1) * dh)
        y = _rms(u_ref[:, sl].astype(F32)) * g
        o_ref[:, sl] = (y * factor).astype(BF16)
        kf_ref[:, sl] = y


def _qkpost(u, qk_g, L, n_sub, dh):
    rows = u.shape[0]
    W = n_sub * dh
    tm = _tile(L, 512)
    return pl.pallas_call(
        functools.partial(_qkpost_kernel, n_sub=n_sub, dh=dh),
        grid=(rows // tm, 2),
        in_specs=[pl.BlockSpec((tm, W), lambda i, j: (i, j)),
                  pl.BlockSpec((1, 1, dh), lambda i, j: (j, 0, 0))],
        out_specs=[pl.BlockSpec((tm, W), lambda i, j: (i, j)),
                   pl.BlockSpec((tm, W), lambda i, j: (i, 0))],
        out_shape=[jax.ShapeDtypeStruct((rows, 2 * W), BF16),
                   jax.ShapeDtypeStruct((rows, W), F32)],
        compiler_params=_params("parallel", "arbitrary"),
        name="qkpost",
    )(u, qk_g.reshape(2, 1, dh))


def _attn_kernel(*refs, dh, lam_init, has_ctx, stable):
    if has_ctx:
        lam_ref, sg_ref, q_ref, k_ref, v_ref, g_ref, ck_ref, cv_ref, o_ref = refs
    else:
        lam_ref, sg_ref, q_ref, k_ref, v_ref, g_ref, o_ref = refs
    lm = lam_ref[...]
    lam = (jnp.exp(jnp.sum(lm[0:1] * lm[1:2], axis=1, keepdims=True))
           - jnp.exp(jnp.sum(lm[2:3] * lm[3:4], axis=1, keepdims=True)) + lam_init)
    L = k_ref.shape[0]
    o = None
    for c in range(2):
        sl = slice(c * dh, (c + 1) * dh)
        qc = q_ref[:, sl]
        if stable:
            v = v_ref[...].astype(BF16)
            s1 = lax.dot_general(qc, k_ref[:, sl], NT_DIMS, preferred_element_type=F32)
            m = jnp.max(s1, axis=-1, keepdims=True)
            if has_ctx:
                s2 = lax.dot_general(qc, ck_ref[0, :, sl], NT_DIMS, preferred_element_type=F32)
                m = jnp.maximum(m, jnp.max(s2, axis=-1, keepdims=True))
            p1 = jnp.exp2(s1 - m)
            l = jnp.sum(p1, axis=-1, keepdims=True)
            oc = jnp.dot(p1.astype(BF16), v, preferred_element_type=F32)
            if has_ctx:
                p2 = jnp.exp2(s2 - m)
                l = l + jnp.sum(p2, axis=-1, keepdims=True)
                oc = oc + jnp.dot(p2.astype(BF16), cv_ref[0], preferred_element_type=F32)
        else:
            kb = _tile(L, KEY_BLOCK)
            blocks = [(k_ref[s:s + kb, sl], v_ref[s:s + kb, :]) for s in range(0, L, kb)]
            if has_ctx:
                P = ck_ref.shape[1]
                kb = _tile(P, KEY_BLOCK)
                blocks += [(ck_ref[0, s:s + kb, sl], cv_ref[0, s:s + kb, :]) for s in range(0, P, kb)]
            lanes = None
            oc = None
            for kb, vb in blocks:
                p = jnp.exp2(lax.dot_general(qc, kb, NT_DIMS, preferred_element_type=F32))
                for t in range(0, p.shape[1], LANES):
                    lanes = p[:, t:t + LANES] if lanes is None else lanes + p[:, t:t + LANES]
                ob = jnp.dot(p.astype(BF16), vb.astype(BF16), preferred_element_type=F32)
                oc = ob if oc is None else oc + ob
            l = jnp.sum(lanes, axis=-1, keepdims=True)
        coef = (1.0 / l) if c == 0 else (-lam / l)
        o = oc * coef if o is None else o + oc * coef
    o = _rms(o) * sg_ref[...] * (1.0 - lam_init)
    o_ref[...] = (o * _silu(g_ref[...].astype(F32))).astype(BF16)


def _attention(qk, u, lam_p, subln_g, ctx, B, L, H, dh, lam_init, tq_pref, stable):
    rows = qk.shape[0]
    V = 2 * dh
    tq = _tile(L, tq_pref)
    nq = L // tq
    has_ctx = ctx is not None
    in_specs = [pl.BlockSpec((4, dh), lambda b, h, i: (0, 0)),
                pl.BlockSpec((1, V), lambda b, h, i: (0, 0)),
                pl.BlockSpec((tq, V), lambda b, h, i: (b * nq + i, h)),
                pl.BlockSpec((L, V), lambda b, h, i: (b, H + h)),
                pl.BlockSpec((L, V), lambda b, h, i: (b, 2 * H + h)),
                pl.BlockSpec((tq, V), lambda b, h, i: (b * nq + i, 3 * H + h))]
    args = [lam_p, subln_g.reshape(1, V), qk, qk, u, u]
    if has_ctx:
        P = ctx[0].shape[1]
        in_specs += [pl.BlockSpec((1, P, V), lambda b, h, i: (b, 0, h))] * 2
        args += list(ctx)
    return pl.pallas_call(
        functools.partial(_attn_kernel, dh=dh, lam_init=lam_init, has_ctx=has_ctx, stable=stable),
        grid=(B, H, nq),
        in_specs=in_specs,
        out_specs=pl.BlockSpec((tq, V), lambda b, h, i: (b * nq + i, h)),
        out_shape=jax.ShapeDtypeStruct((rows, H * V), BF16),
        compiler_params=_params("parallel", "parallel", "arbitrary"),
        name="diffattn_stable" if stable else "diffattn",
    )(*args)


def _score_bound(qk_g, dh, cache_k):
    qn = math.sqrt(dh) * jnp.max(jnp.abs(qk_g[0])) * (dh ** -0.5 * LOG2E)
    kn = math.sqrt(dh) * jnp.max(jnp.abs(qk_g[1]))
    if cache_k is not None:
        kn = jnp.maximum(kn, jnp.sqrt(jnp.max(jnp.sum(jnp.square(cache_k.astype(F32)), axis=-1))))
    return 1.01 * qn * kn


def _diff_attention_layer(x2, B, L, p, mods, rows_per_mod, layer_idx, ctx, tiles):
    shift, scale, gate = mods
    dh = p['qk_g'].shape[1]
    V = p['subln_g'].shape[0]
    QKW = (p['in_w'].shape[1] - 2 * p['out_w'].shape[0]) // 2
    H = p['out_w'].shape[0] // V
    n_sub = QKW // dh
    lam_init = 0.8 - 0.6 * math.exp(-0.3 * layer_idx)
    is_prompt = ctx is None
    new = None
    if is_prompt:
        u = _inproj(x2, p['norm_g'], scale, shift, p['in_w'].astype(BF16), rows_per_mod, F32,
                    *tiles['inproj'])
        qk, kf = _qkpost(u, p['qk_g'], L, n_sub, dh)
        new = (kf.reshape(B, L, n_sub, dh), u[:, 2 * QKW:2 * QKW + H * V].reshape(B, L, H, V))
        cache = None
    else:
        u = _inproj_rope(x2, p['norm_g'], scale, shift, p['in_w'].astype(BF16), L, p['qk_g'], dh, QKW,
                         *tiles['inproj'])
        qk = u
        ck, cv = ctx
        P = ck.shape[1]
        cache = (ck.reshape(B, P, QKW).astype(BF16), cv.reshape(B, P, H * V).astype(BF16))
    def attend(stable):
        return _attention(qk, u, p['lam'], p['subln_g'], cache, B, L, H, dh, lam_init, tiles['tq'], stable)

    bound = _score_bound(p['qk_g'], dh, None if is_prompt else ctx[0])
    o = lax.cond(bound <= MAX_UNSHIFTED_SCORE, lambda: attend(False), lambda: attend(True))
    y = _outproj(o, p['out_w'].astype(BF16), x2, gate, rows_per_mod, *tiles['outproj'])
    return y, new


def _ret_kernel(*refs, nc, dk, has_s0, emit_state):
    dec_ref, gn_ref, q_ref, k_ref, v_ref, g_ref = refs[:6]
    i = 6
    if has_s0:
        s0_ref = refs[i]
        i += 1
    o_ref = refs[i]
    i += 1
    if emit_state:
        so_ref = refs[i]
        i += 1
    sf_ref, sb_ref, acc_ref = refs[i:]
    C = CHUNK
    lg = jnp.log1p(-jnp.exp(dec_ref[0]))
    lgf = lg[0:1]
    lgb = lg[1:2]
    ii = lax.broadcasted_iota(jnp.int32, (C, C), 0)
    jj = lax.broadcasted_iota(jnp.int32, (C, C), 1)
    rel = (ii - jj).astype(F32)
    dtot = (jnp.where(rel >= 0, jnp.exp(jnp.maximum(rel, 0.0) * lgf), 0.0)
            + jnp.where(rel <= 0, jnp.exp(jnp.maximum(-rel, 0.0) * lgb), 0.0))
    pos = lax.broadcasted_iota(jnp.int32, (C, 1), 0).astype(F32)
    qdf = jnp.exp((pos + 1.0) * lgf)
    kdf = jnp.exp((C - 1.0 - pos) * lgf)
    qdb = jnp.exp((C - pos) * lgb)
    kdb = jnp.exp(pos * lgb)
    cdf = jnp.exp(C * lgf)
    cdb = jnp.exp(C * lgb)
    scale = dk ** -0.5
    zero_start = (not has_s0) and nc == 1
    if has_s0:
        sf_ref[...] = s0_ref[0, 0, 0]
        sb_ref[...] = s0_ref[0, 1, 0]
    elif not zero_start:
        sf_ref[...] = jnp.zeros_like(sf_ref)
        sb_ref[...] = jnp.zeros_like(sb_ref)
    gn = gn_ref[0]

    def load(c):
        r = pl.multiple_of(c * C, C)
        q = q_ref[pl.ds(r, C), :]
        k = k_ref[pl.ds(r, C), :].astype(F32) * scale
        v = v_ref[pl.ds(r, C), :].astype(BF16)
        return r, q, k, v

    def fwd(c, carry):
        r, q, k, v = load(c)
        att = lax.dot_general(q, k.astype(BF16), NT_DIMS, preferred_element_type=F32) * dtot
        o = jnp.dot(att.astype(BF16), v, preferred_element_type=F32)
        kv = lax.dot_general((k * kdf).astype(BF16), v, TN_DIMS, preferred_element_type=F32)
        if zero_start:
            sf_ref[...] = kv
        else:
            o = o + jnp.dot((q.astype(F32) * qdf).astype(BF16), sf_ref[...].astype(BF16),
                            preferred_element_type=F32)
            sf_ref[...] = cdf * sf_ref[...] + kv
        acc_ref[pl.ds(r, C), :] = o
        return carry

    lax.fori_loop(0, nc, fwd, 0, unroll=min(nc, 4))

    def bwd(t, carry):
        r, q, k, v = load(nc - 1 - t)
        o = acc_ref[pl.ds(r, C), :]
        kv = lax.dot_general((k * kdb).astype(BF16), v, TN_DIMS, preferred_element_type=F32)
        if zero_start:
            sb_ref[...] = kv
        else:
            o = o + jnp.dot((q.astype(F32) * qdb).astype(BF16), sb_ref[...].astype(BF16),
                            preferred_element_type=F32)
            sb_ref[...] = cdb * sb_ref[...] + kv
        g = g_ref[pl.ds(r, C), :].astype(F32)
        o_ref[pl.ds(r, C), :] = (_rms(o) * gn * _silu(g)).astype(BF16)
        return carry

    lax.fori_loop(0, nc, bwd, 0, unroll=min(nc, 4))
    if emit_state:
        so_ref[0, 0, 0] = sf_ref[...]
        so_ref[0, 1, 0] = sb_ref[...]


def _retention_layer(x2, B, L, p, mods, rows_per_mod, ctx, tiles):
    shift, scale, gate = mods
    H, dv = p['gn_g'].shape
    dk = (p['in_w'].shape[1] - 2 * H * dv) // (2 * H)
    nc = L // CHUNK
    rows = x2.shape[0]
    has_s0 = ctx is not None
    emit_state = not has_s0
    u = _inproj(x2, p['norm_g'], scale, shift, p['in_w'].astype(BF16), rows_per_mod, BF16,
                *tiles['inproj'])
    kq = (H * dk) // dk
    kv = (2 * H * dk) // dv
    in_specs = [pl.BlockSpec((1, 2, 1), lambda b, h: (h, 0, 0)),
                pl.BlockSpec((1, 1, dv), lambda b, h: (h, 0, 0)),
                pl.BlockSpec((L, dk), lambda b, h: (b, h)),
                pl.BlockSpec((L, dk), lambda b, h: (b, kq + h)),
                pl.BlockSpec((L, dv), lambda b, h: (b, kv + h)),
                pl.BlockSpec((L, dv), lambda b, h: (b, kv + H + h))]
    args = [p['decay'].T.reshape(H, 2, 1), p['gn_g'].reshape(H, 1, dv), u, u, u, u]
    st_spec = pl.BlockSpec((1, 2, 1, dk, dv), lambda b, h: (b, 0, h, 0, 0))
    if has_s0:
        in_specs.append(st_spec)
        args.append(ctx[0])
    out_specs = [pl.BlockSpec((L, dv), lambda b, h: (b, h))]
    out_shape = [jax.ShapeDtypeStruct((rows, H * dv), BF16)]
    if emit_state:
        out_specs.append(st_spec)
        out_shape.append(jax.ShapeDtypeStruct((B, 2, H, dk, dv), F32))
    res = pl.pallas_call(
        functools.partial(_ret_kernel, nc=nc, dk=dk, has_s0=has_s0, emit_state=emit_state),
        grid=(B, H),
        in_specs=in_specs,
        out_specs=out_specs,
        out_shape=out_shape,
        scratch_shapes=[pltpu.VMEM((dk, dv), F32), pltpu.VMEM((dk, dv), F32), pltpu.VMEM((L, dv), F32)],
        compiler_params=_params("parallel", "parallel"),
        name="retention",
    )(*args)
    y = _outproj(res[0], p['out_w'].astype(BF16), x2, gate, rows_per_mod, *tiles['outproj'])
    return y, ((res[1],) if emit_state else None)


def _mlpre_kernel(xm_ref, cw_ref, cb_ref, wbd_ref, gw_ref, xc_ref, q_ref, k_ref, v_ref, gt_ref, *, bw):
    L, tc = xm_ref.shape
    x = xm_ref[...].astype(F32)
    cw = cw_ref[...]
    W = cw.shape[0]
    pad = (W - 1) // 2
    row = lax.broadcasted_iota(jnp.int32, (L, tc), 0)
    acc = x * cw[pad:pad + 1] + cb_ref[...]
    for w in range(W):
        d = w - pad
        if d == 0:
            continue
        sh = pltpu.roll(x, (-d) % L, 0)
        valid = (row < L - d) if d > 0 else (row >= -d)
        acc = acc + jnp.where(valid, sh, 0.0) * cw[w:w + 1]
    xc = _silu(acc)
    xc_ref[...] = xc.astype(BF16)
    xcb = xc.astype(BF16)
    xmb = xm_ref[...].astype(BF16)
    gsum = jnp.zeros(gt_ref.shape, F32)
    for blk in range(tc // bw):
        sl = slice(blk * bw, (blk + 1) * bw)
        q = jnp.dot(xcb[:, sl], wbd_ref[0, blk], preferred_element_type=F32).astype(BF16)
        k = jnp.dot(xcb[:, sl], wbd_ref[1, blk], preferred_element_type=F32).astype(BF16)
        v = jnp.dot(xmb[:, sl], wbd_ref[2, blk], preferred_element_type=F32).astype(BF16)
        q_ref[:, sl] = q
        k_ref[:, sl] = k
        v_ref[:, sl] = v
        gsum = gsum + jnp.dot(q, gw_ref[0, sl, :], preferred_element_type=F32)
        gsum = gsum + jnp.dot(k, gw_ref[1, sl, :], preferred_element_type=F32)
        gsum = gsum + jnp.dot(v, gw_ref[2, sl, :], preferred_element_type=F32)

    @pl.when(pl.program_id(1) == 0)
    def _():
        gt_ref[...] = gsum

    @pl.when(pl.program_id(1) != 0)
    def _():
        gt_ref[...] = gt_ref[...] + gsum


def _logsigmoid(x):
    return jnp.minimum(x, 0.0) - jnp.log1p(jnp.exp(-jnp.abs(x)))


def _mlscan_chunk(gl_ref, gb_ref, q_ref, k_ref, v_ref, h_ref, c_ref, n_ref, m_ref, d, dk, zero_state):
    C = CHUNK
    gl = gl_ref[0, 0, 0, 0] + gb_ref[0, 0]
    ih_row = gl[0:1]
    f_row = _logsigmoid(gl[1:2])
    ii = lax.broadcasted_iota(jnp.int32, (C, C), 0)
    jj = lax.broadcasted_iota(jnp.int32, (C, C), 1)
    sgn = 1 - 2 * d
    seen = (ii - jj) * sgn >= 0
    seen_t = (jj - ii) * sgn >= 0
    eye = ii == jj
    fmat = jnp.broadcast_to(f_row, (C, C))
    imat = jnp.broadcast_to(ih_row, (C, C))
    bh_col = jnp.sum(jnp.where(seen, fmat, 0.0), axis=1, keepdims=True)
    f_col = jnp.sum(jnp.where(eye, fmat, 0.0), axis=1, keepdims=True)
    ih_col = jnp.sum(jnp.where(eye, imat, 0.0), axis=1, keepdims=True)
    bh_row = jnp.sum(jnp.where(seen_t, jnp.broadcast_to(f_col, (C, C)), 0.0), axis=0, keepdims=True)
    tot = jnp.sum(f_row, axis=1, keepdims=True)
    m_prev = jnp.zeros((1, 1), F32) if zero_state else m_ref[...]
    dlog = jnp.where(seen, bh_col - bh_row + ih_row, -jnp.inf)
    inter = bh_col + m_prev
    m_i = jnp.maximum(jnp.max(dlog, axis=1, keepdims=True), inter)
    w = jnp.exp(dlog - m_i)
    wi = jnp.exp(inter - m_i)
    q = q_ref[...]
    kscale = dk ** -0.5
    s = lax.dot_general(q, k_ref[...], NT_DIMS, preferred_element_type=F32) * (w * kscale)
    sb = s.astype(BF16)
    den = jnp.sum(s, axis=1, keepdims=True)
    if not zero_state:
        den = den + wi * jnp.sum(q.astype(F32) * n_ref[...], axis=1, keepdims=True)
    inv_norm = 1.0 / jnp.maximum(jnp.abs(den), jnp.exp(-m_i))
    kw = tot - bh_col + ih_col
    m_new = jnp.maximum(tot + m_prev, jnp.max(kw, axis=0, keepdims=True))
    wk = jnp.exp(kw - m_new)
    dec = jnp.exp(tot + m_prev - m_new)
    kcw = k_ref[...].astype(F32) * (wk * kscale)
    kcb = kcw.astype(BF16)
    slab = _tile(dk, STATE_SLAB)
    for t in range(0, dk, slab):
        cols = slice(t, t + slab)
        vt = v_ref[:, cols]
        num = jnp.dot(sb, vt, preferred_element_type=F32)
        kv = lax.dot_general(kcb, vt, TN_DIMS, preferred_element_type=F32)
        if zero_state:
            c_ref[:, cols] = kv
        else:
            num = num + wi * jnp.dot(q, c_ref[:, cols].astype(BF16), preferred_element_type=F32)
            c_ref[:, cols] = dec * c_ref[:, cols] + kv
        h_ref[0, :, cols] = (num * inv_norm).astype(h_ref.dtype)
    ksum = jnp.sum(kcw, axis=0, keepdims=True)
    n_ref[...] = ksum if zero_state else dec * n_ref[...] + ksum
    m_ref[...] = m_new


def _mlscan_kernel(*refs, nblk, cps, dk, has_s0, emit_state):
    gl_ref, gb_ref, q_ref, k_ref, v_ref = refs[:5]
    i = 5
    if has_s0:
        c0_ref, n0_ref, m0_ref = refs[i:i + 3]
        i += 3
    h_ref = refs[i]
    i += 1
    if emit_state:
        c_ref, n_ref, m_ref = (r.at[0, 0, 0] for r in refs[i:i + 3])
    else:
        c_ref, n_ref, m_ref = refs[i:]
    d = pl.program_id(2)
    c = pl.program_id(3)
    static_zero_start = (not has_s0) and nblk == 1

    if not static_zero_start:
        @pl.when(c == 0)
        def _():
            if has_s0:
                c_ref[...] = c0_ref[0, 0, 0]
                n_ref[...] = n0_ref[0, 0, 0]
                m_ref[...] = m0_ref[0, 0, 0]
            else:
                c_ref[...] = jnp.zeros(c_ref.shape, F32)
                n_ref[...] = jnp.zeros(n_ref.shape, F32)
                m_ref[...] = jnp.zeros(m_ref.shape, F32)

    for step in range(cps):
        j = step + d * (cps - 1 - 2 * step)
        rows = pl.ds(pl.multiple_of(j * CHUNK, CHUNK), CHUNK)
        _mlscan_chunk(gl_ref.at[:, :, :, pl.ds(j, 1)], gb_ref, q_ref.at[rows, :], k_ref.at[rows, :],
                      v_ref.at[rows, :], h_ref.at[:, rows, :], c_ref, n_ref, m_ref, d, dk,
                      zero_state=static_zero_start and step == 0)


def _mlpost_kernel(h_ref, xc_ref, z_ref, gn_ref, sk_ref, o_ref):
    o = _rms(h_ref[0].astype(F32) + h_ref[1].astype(F32)) * gn_ref[...]
    o = (o + sk_ref[...] * xc_ref[...].astype(F32)) * _silu(z_ref[...].astype(F32))
    o_ref[...] = o.astype(BF16)


def _mlstm_layer(x2, B, L, p, mods, rows_per_mod, ctx, tiles):
    shift, scale, gate = mods
    H, dk = p['gn_g'].shape
    Wd = H * dk
    bsz = p['qkv_bd'].shape[-1]
    nc = L // CHUNK
    rows = x2.shape[0]
    has_s0 = ctx is not None
    emit_state = not has_s0
    u = _inproj(x2, p['norm_g'], scale, shift, p['in_w'].astype(BF16), rows_per_mod, BF16,
                *tiles['inproj'])

    bw = 256 if Wd % 256 == 0 else LANES
    nb = Wd // bw
    per = bw // bsz
    bd = p['qkv_bd'].reshape(3, nb, per, bsz, bsz)
    eye = jnp.eye(per, dtype=F32)
    wbd = jnp.einsum('tngio,gh->tngiho', bd, eye).reshape(3, nb, bw, bw).astype(BF16)
    n_gate = p['gate_w'].shape[-1]
    gw = p['gate_w'].reshape(2, 3, Wd, n_gate).transpose(1, 2, 0, 3).reshape(3, Wd, 2 * n_gate)
    gw = jnp.pad(gw, ((0, 0), (0, 0), (0, LANES - 2 * n_gate))).astype(BF16)

    tc = _tile(Wd, 512)
    nbt = tc // bw
    xc, q, k, v, gates = pl.pallas_call(
        functools.partial(_mlpre_kernel, bw=bw),
        grid=(B, Wd // tc),
        in_specs=[pl.BlockSpec((L, tc), lambda b, j: (b, j)),
                  pl.BlockSpec((p['conv_w'].shape[0], tc), lambda b, j: (0, j)),
                  pl.BlockSpec((1, tc), lambda b, j: (0, j)),
                  pl.BlockSpec((3, nbt, bw, bw), lambda b, j: (0, j, 0, 0)),
                  pl.BlockSpec((3, tc, LANES), lambda b, j: (0, j, 0))],
        out_specs=[pl.BlockSpec((L, tc), lambda b, j: (b, j))] * 4
                  + [pl.BlockSpec((L, LANES), lambda b, j: (b, 0))],
        out_shape=[jax.ShapeDtypeStruct((rows, Wd), BF16)] * 4
                  + [jax.ShapeDtypeStruct((rows, LANES), F32)],
        compiler_params=_params("parallel", "arbitrary"),
        name="mlstm_pre",
    )(u, p['conv_w'], p['conv_b'].reshape(1, Wd), wbd, gw)

    gl = gates[:, :2 * n_gate].reshape(B, nc, CHUNK, 2, 2, H).transpose(0, 3, 5, 1, 4, 2)
    gb = p['gate_b'].reshape(2, 2, H).transpose(0, 2, 1).reshape(2, H, 2, 1)

    cps = _tile(nc, SCAN_CHUNKS_PER_STEP)
    nblk = nc // cps
    blk = cps * CHUNK

    def block_of(d, c):
        return c + d * (nblk - 1 - 2 * c)

    tok = lambda b, h, d, c: (b * nblk + block_of(d, c), h)
    in_specs = [pl.BlockSpec((1, 1, 1, cps, 2, CHUNK), lambda b, h, d, c: (b, d, h, block_of(d, c), 0, 0)),
                pl.BlockSpec((1, 1, 2, 1), lambda b, h, d, c: (d, h, 0, 0)),
                pl.BlockSpec((blk, dk), tok),
                pl.BlockSpec((blk, dk), tok),
                pl.BlockSpec((blk, dk), tok)]
    args = [gl, gb, q, k, v]
    c_spec = pl.BlockSpec((1, 1, 1, dk, dk), lambda b, h, d, c: (b, d, h, 0, 0))
    n_spec = pl.BlockSpec((1, 1, 1, 1, dk), lambda b, h, d, c: (b, d, h, 0, 0))
    m_spec = pl.BlockSpec((1, 1, 1, 1, 1), lambda b, h, d, c: (b, d, h, 0, 0))
    if has_s0:
        c0, n0, m0 = ctx
        in_specs += [c_spec, n_spec, m_spec]
        args += [c0, n0.reshape(B, 2, H, 1, dk), m0.reshape(B, 2, H, 1, 1)]
    out_specs = [pl.BlockSpec((1, blk, dk), lambda b, h, d, c: (d, b * nblk + block_of(d, c), h))]
    out_shape = [jax.ShapeDtypeStruct((2, rows, Wd), BF16)]
    if emit_state:
        out_specs += [c_spec, n_spec, m_spec]
        out_shape += [jax.ShapeDtypeStruct((B, 2, H, dk, dk), F32),
                      jax.ShapeDtypeStruct((B, 2, H, 1, dk), F32),
                      jax.ShapeDtypeStruct((B, 2, H, 1, 1), F32)]
    res = pl.pallas_call(
        functools.partial(_mlscan_kernel, nblk=nblk, cps=cps, dk=dk, has_s0=has_s0, emit_state=emit_state),
        grid=(B, H, 2, nblk),
        in_specs=in_specs,
        out_specs=out_specs,
        out_shape=out_shape,
        scratch_shapes=([] if emit_state else
                        [pltpu.VMEM((dk, dk), F32), pltpu.VMEM((1, dk), F32), pltpu.VMEM((1, 1), F32)]),
        compiler_params=_params("parallel", "parallel", "arbitrary", "arbitrary"),
        name="mlstm_scan",
    )(*args)
    hdir = res[0]

    tm = _tile(rows, 1024)
    o = pl.pallas_call(
        _mlpost_kernel,
        grid=(rows // tm, H),
        in_specs=[pl.BlockSpec((2, tm, dk), lambda i, h: (0, i, h)),
                  pl.BlockSpec((tm, dk), lambda i, h: (i, h)),
                  pl.BlockSpec((tm, dk), lambda i, h: (i, H + h)),
                  pl.BlockSpec((1, dk), lambda i, h: (0, h)),
                  pl.BlockSpec((1, dk), lambda i, h: (0, h))],
        out_specs=pl.BlockSpec((tm, dk), lambda i, h: (i, h)),
        out_shape=jax.ShapeDtypeStruct((rows, Wd), BF16),
        compiler_params=_params("parallel", "arbitrary"),
        name="mlstm_post",
    )(hdir, xc, u, p['gn_g'].reshape(1, Wd), p['skip'].reshape(1, Wd))
    y = _outproj(o, p['out_w'].astype(BF16), x2, gate, rows_per_mod, *tiles['outproj'])
    new = None
    if emit_state:
        new = (res[1], res[2].reshape(B, 2, H, dk), res[3].reshape(B, 2, H))
    return y, new


_BASE = dict(inproj=(1024, 2048), outproj=(512, 2048), tq=512)
_F32_OUT = dict(_BASE, inproj=(1024, 1024))
TILES_PROMPT = [_F32_OUT, _BASE, _BASE, _F32_OUT]
TILES_SAMPLE = [_BASE, _BASE, _BASE, _BASE]

def kernel(x_prompt, x_sample, c, c_ctx, cache_k_l0, cache_v_l0, state_ret_l1, state_C_l2, state_n_l2, state_m_l2, cache_k_l3, cache_v_l3, norm_g_l0, ada_w_l0, ada_b_l0, in_w_l0, out_w_l0, qk_g_l0, lam_l0, subln_g_l0, norm_g_l1, ada_w_l1, ada_b_l1, in_w_l1, out_w_l1, decay_l1, gn_g_l1, norm_g_l2, ada_w_l2, ada_b_l2, in_w_l2, out_w_l2, conv_w_l2, conv_b_l2, qkv_bd_l2, gate_w_l2, gate_b_l2, gn_g_l2, skip_l2, norm_g_l3, ada_w_l3, ada_b_l3, in_w_l3, out_w_l3, qk_g_l3, lam_l3, subln_g_l3):
    layers = [
        dict(norm_g=norm_g_l0, ada_w=ada_w_l0, ada_b=ada_b_l0, in_w=in_w_l0, out_w=out_w_l0,
             qk_g=qk_g_l0, lam=lam_l0, subln_g=subln_g_l0),
        dict(norm_g=norm_g_l1, ada_w=ada_w_l1, ada_b=ada_b_l1, in_w=in_w_l1, out_w=out_w_l1,
             decay=decay_l1, gn_g=gn_g_l1),
        dict(norm_g=norm_g_l2, ada_w=ada_w_l2, ada_b=ada_b_l2, in_w=in_w_l2, out_w=out_w_l2,
             conv_w=conv_w_l2, conv_b=conv_b_l2, qkv_bd=qkv_bd_l2, gate_w=gate_w_l2, gate_b=gate_b_l2,
             gn_g=gn_g_l2, skip=skip_l2),
        dict(norm_g=norm_g_l3, ada_w=ada_w_l3, ada_b=ada_b_l3, in_w=in_w_l3, out_w=out_w_l3,
             qk_g=qk_g_l3, lam=lam_l3, subln_g=subln_g_l3),
    ]
    ctxs = [(cache_k_l0, cache_v_l0), (state_ret_l1,), (state_C_l2, state_n_l2, state_m_l2),
            (cache_k_l3, cache_v_l3)]
    Bp, Lp, D = x_prompt.shape
    Bs, Ls, _ = x_sample.shape
    xp = x_prompt.reshape(Bp * Lp, D)
    xs = x_sample.reshape(Bs * Ls, D)
    n_cond = 1 + Bs
    cond_rows = -(-n_cond // 8) * 8
    cond = jnp.concatenate([c_ctx[None, :], c, jnp.zeros((cond_rows - n_cond, D), F32)], axis=0)
    new_states = []
    for i, p in enumerate(layers):
        m = _adaln(cond, p['ada_w'], p['ada_b'])
        mods_p = tuple(m[0:1, k * D:(k + 1) * D].reshape(1, 1, D) for k in range(3))
        mods_s = tuple(m[1:n_cond, k * D:(k + 1) * D].reshape(Bs, 1, D) for k in range(3))
        kind = i % 3
        tp, ts = TILES_PROMPT[i], TILES_SAMPLE[i]
        if kind == 0:
            xp, st = _diff_attention_layer(xp, Bp, Lp, p, mods_p, Bp * Lp, i, None, tp)
            xs, _ = _diff_attention_layer(xs, Bs, Ls, p, mods_s, Ls, i, ctxs[i], ts)
        elif kind == 1:
            xp, st = _retention_layer(xp, Bp, Lp, p, mods_p, Bp * Lp, None, tp)
            xs, _ = _retention_layer(xs, Bs, Ls, p, mods_s, Ls, ctxs[i], ts)
        else:
            xp, st = _mlstm_layer(xp, Bp, Lp, p, mods_p, Bp * Lp, None, tp)
            xs, _ = _mlstm_layer(xs, Bs, Ls, p, mods_s, Ls, ctxs[i], ts)
        new_states.append(st)
    (k_l0, v_l0), (ret_l1,), (C_l2, n_l2, m_l2), (k_l3, v_l3) = new_states
    return (xp.reshape(Bp, Lp, D), xs.reshape(Bs, Ls, D), k_l0, v_l0, ret_l1, C_l2, n_l2, m_l2, k_l3, v_l3)
```

```python
import functools
import math

import jax
import jax.numpy as jnp
from jax import lax
from jax.experimental import pallas as pl
from jax.experimental.pallas import tpu as pltpu

F32 = jnp.float32
BF16 = jnp.bfloat16

CHUNK = 256
GRID_W = 64
ROPE_BASE = 10000.0
NORM_EPS = 1e-6
LANES = 128
VMEM_LIMIT_BYTES = 56 * 1024 * 1024

LOG2E = math.log2(math.e)
MAX_UNSHIFTED_SCORE = 60.0
KEY_BLOCK = 512
PRE_ROWS = 2048
STATE_SLAB = 256
SCAN_CHUNKS_PER_STEP = 4

NT_DIMS = (((1,), (1,)), ((), ()))
TN_DIMS = (((0,), (0,)), ((), ()))


def _tile(n, pref):
    if n <= pref:
        return n
    t = pref
    while n % t:
        t -= 1
    return t


def _params(*sem):
    return pltpu.CompilerParams(dimension_semantics=sem, vmem_limit_bytes=VMEM_LIMIT_BYTES)


def _silu(x):
    return x * jax.nn.sigmoid(x)


def _rms(x):
    return x * lax.rsqrt(jnp.mean(x * x, axis=-1, keepdims=True) + NORM_EPS)


def _adaln_kernel(c_ref, w_ref, b_ref, o_ref):
    s = _silu(c_ref[...]).astype(BF16)
    o_ref[...] = jnp.dot(s, w_ref[...].astype(BF16), preferred_element_type=F32) + b_ref[...]


def _adaln(cond, w, b):
    R, D = cond.shape
    N = w.shape[1]
    tn = _tile(N, 512)
    return pl.pallas_call(
        _adaln_kernel,
        grid=(N // tn,),
        in_specs=[pl.BlockSpec((R, D), lambda j: (0, 0)),
                  pl.BlockSpec((D, tn), lambda j: (0, j)),
                  pl.BlockSpec((1, tn), lambda j: (0, j))],
        out_specs=pl.BlockSpec((R, tn), lambda j: (0, j)),
        out_shape=jax.ShapeDtypeStruct((R, N), F32),
        compiler_params=_params("arbitrary"),
        name="adaln",
    )(cond, w, b.reshape(1, N))


def _inproj_kernel(x_ref, g_ref, sc_ref, sh_ref, w_ref, o_ref, h_ref):
    @pl.when(pl.program_id(1) == 0)
    def _():
        gain = g_ref[...] * (1.0 + sc_ref[0])
        h_ref[...] = (_rms(x_ref[...]) * gain + sh_ref[0]).astype(BF16)

    o_ref[...] = jnp.dot(h_ref[...], w_ref[...], preferred_element_type=F32).astype(o_ref.dtype)


def _inproj(x2, norm_g, scale, shift, w, rows_per_mod, out_dtype, tm_pref=1024, tn_pref=512):
    rows, D = x2.shape
    N = w.shape[1]
    tm = _tile(rows_per_mod, tm_pref)
    tn = _tile(N, tn_pref)
    mod = lambda i, j: ((i * tm) // rows_per_mod, 0, 0)
    return pl.pallas_call(
        _inproj_kernel,
        grid=(rows // tm, N // tn),
        in_specs=[pl.BlockSpec((tm, D), lambda i, j: (i, 0)),
                  pl.BlockSpec((1, D), lambda i, j: (0, 0)),
                  pl.BlockSpec((1, 1, D), mod),
                  pl.BlockSpec((1, 1, D), mod),
                  pl.BlockSpec((D, tn), lambda i, j: (0, j))],
        out_specs=pl.BlockSpec((tm, tn), lambda i, j: (i, j)),
        out_shape=jax.ShapeDtypeStruct((rows, N), out_dtype),
        scratch_shapes=[pltpu.VMEM((tm, D), BF16)],
        compiler_params=_params("parallel", "arbitrary"),
        name="inproj",
    )(x2, norm_g.reshape(1, D), scale, shift, w)


def _outproj_kernel(a_ref, w_ref, x_ref, gt_ref, o_ref):
    y = jnp.dot(a_ref[...], w_ref[...], preferred_element_type=F32)
    o_ref[...] = x_ref[...] + gt_ref[0] * y


def _outproj(a, w, x2, gate, rows_per_mod, tm_pref=1024, tn_pref=512):
    rows, K = a.shape
    D = w.shape[1]
    tm = _tile(rows_per_mod, tm_pref)
    tn = _tile(D, tn_pref)
    w_mode = dict(pipeline_mode=pl.Buffered(1)) if tn == D else {}
    return pl.pallas_call(
        _outproj_kernel,
        grid=(rows // tm, D // tn),
        in_specs=[pl.BlockSpec((tm, K), lambda i, j: (i, 0)),
                  pl.BlockSpec((K, tn), lambda i, j: (0, j), **w_mode),
                  pl.BlockSpec((tm, tn), lambda i, j: (i, j)),
                  pl.BlockSpec((1, 1, tn), lambda i, j: ((i * tm) // rows_per_mod, 0, j))],
        out_specs=pl.BlockSpec((tm, tn), lambda i, j: (i, j)),
        out_shape=jax.ShapeDtypeStruct((rows, D), F32),
        compiler_params=_params("parallel", "arbitrary"),
        name="outproj",
    )(a, w, x2, gate)


def _rope_tables(L, dh):
    t = jnp.arange(L)
    row = (t // GRID_W).astype(F32)
    col = (t % GRID_W).astype(F32)
    n_freq = dh // 4
    inv = ROPE_BASE ** (-jnp.arange(n_freq, dtype=F32) / n_freq)
    ar = row[:, None] * inv[None, :]
    ac = col[:, None] * inv[None, :]
    cos = jnp.concatenate([jnp.cos(ar), jnp.cos(ar), jnp.cos(ac), jnp.cos(ac)], axis=-1)
    sin = jnp.concatenate([-jnp.sin(ar), jnp.sin(ar), -jnp.sin(ac), jnp.sin(ac)], axis=-1)
    return cos.astype(F32), sin.astype(F32)


def _rope_gain_tables(L, dh, qk_g):
    cos, sin = _rope_tables(L, dh)
    scale = jnp.array([dh ** -0.5 * LOG2E, 1.0], F32)[:, None]
    g = qk_g.astype(F32) * scale
    g_swapped = g.reshape(2, 2, 2, dh // 4)[:, :, ::-1, :].reshape(2, dh)
    return g[:, None, :] * cos[None], g_swapped[:, None, :] * sin[None]


def _qkpost_kernel(u_ref, uv_ref, g_ref, o_ref, kf_ref, vf_ref, *, n_sub, dh, n_head, dv):
    g = g_ref[0]
    j = pl.program_id(1)
    factor = jnp.where(j == 0, dh ** -0.5 * LOG2E, 1.0)
    for s in range(n_sub):
        sl = slice(s * dh, (s + 1) * dh)
        y = _rms(u_ref[:, sl].astype(F32)) * g
        o_ref[:, sl] = (y * factor).astype(BF16)
        kf_ref[:, s, :] = y

    @pl.when(j == 0)
    def _():
        for h in range(n_head):
            vf_ref[:, h, :] = uv_ref[:, h * dv:(h + 1) * dv]


def _qkpost(u, qk_g, L, n_sub, dh, n_head, dv):
    rows = u.shape[0]
    W = n_sub * dh
    Wv = n_head * dv
    tm = _tile(L, 512)
    return pl.pallas_call(
        functools.partial(_qkpost_kernel, n_sub=n_sub, dh=dh, n_head=n_head, dv=dv),
        grid=(rows // tm, 2),
        in_specs=[pl.BlockSpec((tm, W), lambda i, j: (i, j)),
                  pl.BlockSpec((tm, Wv), lambda i, j: (i, 2 * W // Wv)),
                  pl.BlockSpec((1, 1, dh), lambda i, j: (j, 0, 0))],
        out_specs=[pl.BlockSpec((tm, W), lambda i, j: (i, j)),
                   pl.BlockSpec((tm, n_sub, dh), lambda i, j: (i, 0, 0)),
                   pl.BlockSpec((tm, n_head, dv), lambda i, j: (i, 0, 0))],
        out_shape=[jax.ShapeDtypeStruct((rows, 2 * W), BF16),
                   jax.ShapeDtypeStruct((rows, n_sub, dh), F32),
                   jax.ShapeDtypeStruct((rows, n_head, dv), F32)],
        compiler_params=_params("parallel", "arbitrary"),
        name="qkpost",
    )(u, u, qk_g.reshape(2, 1, dh))


def _qkrope_kernel(u_ref, t1_ref, t2_ref, o_ref, *, n_sub, dh):
    t1 = t1_ref[0]
    t2 = t2_ref[0]
    lane = lax.broadcasted_iota(jnp.int32, t1.shape, 1)
    first = (lane % (dh // 2)) < (dh // 4)
    for s in range(n_sub):
        sl = slice(s * dh, (s + 1) * dh)
        x = u_ref[:, sl].astype(F32)
        r = lax.rsqrt(jnp.mean(x * x, axis=-1, keepdims=True) + NORM_EPS)
        swapped = jnp.where(first, pltpu.roll(x, dh - dh // 4, 1), pltpu.roll(x, dh // 4, 1))
        o_ref[:, sl] = ((x * t1 + swapped * t2) * r).astype(BF16)


def _qkrope(u, qk_g, L, n_sub, dh):
    rows = u.shape[0]
    W = n_sub * dh
    tm = _tile(L, 512)
    nl = L // tm
    t1, t2 = _rope_gain_tables(L, dh, qk_g)
    tab = pl.BlockSpec((1, tm, dh), lambda i, j: (j, i % nl, 0))
    return pl.pallas_call(
        functools.partial(_qkrope_kernel, n_sub=n_sub, dh=dh),
        grid=(rows // tm, 2),
        in_specs=[pl.BlockSpec((tm, W), lambda i, j: (i, j)), tab, tab],
        out_specs=pl.BlockSpec((tm, W), lambda i, j: (i, j)),
        out_shape=jax.ShapeDtypeStruct((rows, 2 * W), BF16),
        compiler_params=_params("parallel", "arbitrary"),
        name="qkrope",
    )(u, t1, t2)


def _attn_kernel(*refs, dh, lam_init, has_ctx, stable):
    if has_ctx:
        lam_ref, sg_ref, q_ref, k_ref, v_ref, g_ref, ck_ref, cv_ref, o_ref = refs
    else:
        lam_ref, sg_ref, q_ref, k_ref, v_ref, g_ref, o_ref = refs
    lm = lam_ref[...]
    lam = (jnp.exp(jnp.sum(lm[0:1] * lm[1:2], axis=1, keepdims=True))
           - jnp.exp(jnp.sum(lm[2:3] * lm[3:4], axis=1, keepdims=True)) + lam_init)
    L = k_ref.shape[0]
    o = None
    for c in range(2):
        sl = slice(c * dh, (c + 1) * dh)
        qc = q_ref[:, sl]
        if stable:
            v = v_ref[...].astype(BF16)
            s1 = lax.dot_general(qc, k_ref[:, sl], NT_DIMS, preferred_element_type=F32)
            m = jnp.max(s1, axis=-1, keepdims=True)
            if has_ctx:
                s2 = lax.dot_general(qc, ck_ref[0, :, sl], NT_DIMS, preferred_element_type=F32)
                m = jnp.maximum(m, jnp.max(s2, axis=-1, keepdims=True))
            p1 = jnp.exp2(s1 - m)
            l = jnp.sum(p1, axis=-1, keepdims=True)
            oc = jnp.dot(p1.astype(BF16), v, preferred_element_type=F32)
            if has_ctx:
                p2 = jnp.exp2(s2 - m)
                l = l + jnp.sum(p2, axis=-1, keepdims=True)
                oc = oc + jnp.dot(p2.astype(BF16), cv_ref[0], preferred_element_type=F32)
        else:
            kb = _tile(L, KEY_BLOCK)
            blocks = [(k_ref[s:s + kb, sl], v_ref[s:s + kb, :]) for s in range(0, L, kb)]
            if has_ctx:
                P = ck_ref.shape[1]
                kb = _tile(P, KEY_BLOCK)
                blocks += [(ck_ref[0, s:s + kb, sl], cv_ref[0, s:s + kb, :]) for s in range(0, P, kb)]
            lanes = None
            oc = None
            for kb, vb in blocks:
                p = jnp.exp2(lax.dot_general(qc, kb, NT_DIMS, preferred_element_type=F32))
                for t in range(0, p.shape[1], LANES):
                    lanes = p[:, t:t + LANES] if lanes is None else lanes + p[:, t:t + LANES]
                ob = jnp.dot(p.astype(BF16), vb.astype(BF16), preferred_element_type=F32)
                oc = ob if oc is None else oc + ob
            l = jnp.sum(lanes, axis=-1, keepdims=True)
        coef = (1.0 / l) if c == 0 else (-lam / l)
        o = oc * coef if o is None else o + oc * coef
    o = _rms(o) * sg_ref[...] * (1.0 - lam_init)
    o_ref[...] = (o * _silu(g_ref[...].astype(F32))).astype(BF16)


def _attention(qk, u, lam_p, subln_g, ctx, B, L, H, dh, lam_init, tq_pref, stable):
    rows = qk.shape[0]
    V = 2 * dh
    tq = _tile(L, tq_pref)
    nq = L // tq
    has_ctx = ctx is not None
    in_specs = [pl.BlockSpec((4, dh), lambda b, h, i: (0, 0)),
                pl.BlockSpec((1, V), lambda b, h, i: (0, 0)),
                pl.BlockSpec((tq, V), lambda b, h, i: (b * nq + i, h)),
                pl.BlockSpec((L, V), lambda b, h, i: (b, H + h)),
                pl.BlockSpec((L, V), lambda b, h, i: (b, 2 * H + h)),
                pl.BlockSpec((tq, V), lambda b, h, i: (b * nq + i, 3 * H + h))]
    args = [lam_p, subln_g.reshape(1, V), qk, qk, u, u]
    if has_ctx:
        P = ctx[0].shape[1]
        in_specs += [pl.BlockSpec((1, P, V), lambda b, h, i: (b, 0, h))] * 2
        args += list(ctx)
    return pl.pallas_call(
        functools.partial(_attn_kernel, dh=dh, lam_init=lam_init, has_ctx=has_ctx, stable=stable),
        grid=(B, H, nq),
        in_specs=in_specs,
        out_specs=pl.BlockSpec((tq, V), lambda b, h, i: (b * nq + i, h)),
        out_shape=jax.ShapeDtypeStruct((rows, H * V), BF16),
        compiler_params=_params("parallel", "parallel", "arbitrary"),
        name="diffattn_stable" if stable else "diffattn",
    )(*args)


def _score_bound(qk_g, dh, cache_k):
    qn = math.sqrt(dh) * jnp.max(jnp.abs(qk_g[0])) * (dh ** -0.5 * LOG2E)
    kn = math.sqrt(dh) * jnp.max(jnp.abs(qk_g[1]))
    if cache_k is not None:
        kn = jnp.maximum(kn, jnp.sqrt(jnp.max(jnp.sum(jnp.square(cache_k.astype(F32)), axis=-1))))
    return 1.01 * qn * kn


def _diff_attention_layer(x2, B, L, p, mods, rows_per_mod, layer_idx, ctx, tiles):
    shift, scale, gate = mods
    dh = p['qk_g'].shape[1]
    V = p['subln_g'].shape[0]
    QKW = (p['in_w'].shape[1] - 2 * p['out_w'].shape[0]) // 2
    H = p['out_w'].shape[0] // V
    n_sub = QKW // dh
    lam_init = 0.8 - 0.6 * math.exp(-0.3 * layer_idx)
    is_prompt = ctx is None
    new = None
    if is_prompt:
        u = _inproj(x2, p['norm_g'], scale, shift, p['in_w'].astype(BF16), rows_per_mod, F32,
                    *tiles['inproj'])
        qk, kf, vf = _qkpost(u, p['qk_g'], L, n_sub, dh, H, V)
        new = (kf.reshape(B, L, n_sub, dh), vf.reshape(B, L, H, V))
        cache = None
    else:
        u = _inproj(x2, p['norm_g'], scale, shift, p['in_w'].astype(BF16), rows_per_mod, BF16,
                    *tiles['inproj'])
        qk = _qkrope(u, p['qk_g'], L, n_sub, dh)
        ck, cv = ctx
        P = ck.shape[1]
        cache = (ck.reshape(B, P, QKW).astype(BF16), cv.reshape(B, P, H * V).astype(BF16))
    def attend(stable):
        return _attention(qk, u, p['lam'], p['subln_g'], cache, B, L, H, dh, lam_init, tiles['tq'], stable)

    bound = _score_bound(p['qk_g'], dh, None if is_prompt else ctx[0])
    o = lax.cond(bound <= MAX_UNSHIFTED_SCORE, lambda: attend(False), lambda: attend(True))
    y = _outproj(o, p['out_w'].astype(BF16), x2, gate, rows_per_mod, *tiles['outproj'])
    return y, new


def _ret_kernel(*refs, nc, dk, has_s0, emit_state):
    dec_ref, gn_ref, q_ref, k_ref, v_ref, g_ref = refs[:6]
    i = 6
    if has_s0:
        s0_ref = refs[i]
        i += 1
    o_ref = refs[i]
    i += 1
    if emit_state:
        so_ref = refs[i]
        i += 1
    sf_ref, sb_ref, acc_ref = refs[i:]
    C = CHUNK
    lg = jnp.log1p(-jnp.exp(dec_ref[0]))
    lgf = lg[0:1]
    lgb = lg[1:2]
    ii = lax.broadcasted_iota(jnp.int32, (C, C), 0)
    jj = lax.broadcasted_iota(jnp.int32, (C, C), 1)
    rel = (ii - jj).astype(F32)
    dtot = (jnp.where(rel >= 0, jnp.exp(jnp.maximum(rel, 0.0) * lgf), 0.0)
            + jnp.where(rel <= 0, jnp.exp(jnp.maximum(-rel, 0.0) * lgb), 0.0))
    pos = lax.broadcasted_iota(jnp.int32, (C, 1), 0).astype(F32)
    qdf = jnp.exp((pos + 1.0) * lgf)
    kdf = jnp.exp((C - 1.0 - pos) * lgf)
    qdb = jnp.exp((C - pos) * lgb)
    kdb = jnp.exp(pos * lgb)
    cdf = jnp.exp(C * lgf)
    cdb = jnp.exp(C * lgb)
    scale = dk ** -0.5
    zero_start = (not has_s0) and nc == 1
    if has_s0:
        sf_ref[...] = s0_ref[0, 0, 0]
        sb_ref[...] = s0_ref[0, 1, 0]
    elif not zero_start:
        sf_ref[...] = jnp.zeros_like(sf_ref)
        sb_ref[...] = jnp.zeros_like(sb_ref)
    gn = gn_ref[0]

    def load(c):
        r = pl.multiple_of(c * C, C)
        q = q_ref[pl.ds(r, C), :]
        k = k_ref[pl.ds(r, C), :].astype(F32) * scale
        v = v_ref[pl.ds(r, C), :].astype(BF16)
        return r, q, k, v

    def fwd(c, carry):
        r, q, k, v = load(c)
        att = lax.dot_general(q, k.astype(BF16), NT_DIMS, preferred_element_type=F32) * dtot
        o = jnp.dot(att.astype(BF16), v, preferred_element_type=F32)
        kv = lax.dot_general((k * kdf).astype(BF16), v, TN_DIMS, preferred_element_type=F32)
        if zero_start:
            sf_ref[...] = kv
        else:
            o = o + jnp.dot((q.astype(F32) * qdf).astype(BF16), sf_ref[...].astype(BF16),
                            preferred_element_type=F32)
            sf_ref[...] = cdf * sf_ref[...] + kv
        acc_ref[pl.ds(r, C), :] = o
        return carry

    lax.fori_loop(0, nc, fwd, 0, unroll=min(nc, 8))

    def bwd(t, carry):
        r, q, k, v = load(nc - 1 - t)
        o = acc_ref[pl.ds(r, C), :]
        kv = lax.dot_general((k * kdb).astype(BF16), v, TN_DIMS, preferred_element_type=F32)
        if zero_start:
            sb_ref[...] = kv
        else:
            o = o + jnp.dot((q.astype(F32) * qdb).astype(BF16), sb_ref[...].astype(BF16),
                            preferred_element_type=F32)
            sb_ref[...] = cdb * sb_ref[...] + kv
        g = g_ref[pl.ds(r, C), :].astype(F32)
        o_ref[pl.ds(r, C), :] = (_rms(o) * gn * _silu(g)).astype(BF16)
        return carry

    lax.fori_loop(0, nc, bwd, 0, unroll=min(nc, 8))
    if emit_state:
        so_ref[0, 0, 0] = sf_ref[...]
        so_ref[0, 1, 0] = sb_ref[...]


def _retention_layer(x2, B, L, p, mods, rows_per_mod, ctx, tiles):
    shift, scale, gate = mods
    H, dv = p['gn_g'].shape
    dk = (p['in_w'].shape[1] - 2 * H * dv) // (2 * H)
    nc = L // CHUNK
    rows = x2.shape[0]
    has_s0 = ctx is not None
    emit_state = not has_s0
    u = _inproj(x2, p['norm_g'], scale, shift, p['in_w'].astype(BF16), rows_per_mod, BF16,
                *tiles['inproj'])
    kq = (H * dk) // dk
    kv = (2 * H * dk) // dv
    in_specs = [pl.BlockSpec((1, 2, 1), lambda b, h: (h, 0, 0)),
                pl.BlockSpec((1, 1, dv), lambda b, h: (h, 0, 0)),
                pl.BlockSpec((L, dk), lambda b, h: (b, h)),
                pl.BlockSpec((L, dk), lambda b, h: (b, kq + h)),
                pl.BlockSpec((L, dv), lambda b, h: (b, kv + h)),
                pl.BlockSpec((L, dv), lambda b, h: (b, kv + H + h))]
    args = [p['decay'].T.reshape(H, 2, 1), p['gn_g'].reshape(H, 1, dv), u, u, u, u]
    st_spec = pl.BlockSpec((1, 2, 1, dk, dv), lambda b, h: (b, 0, h, 0, 0))
    if has_s0:
        in_specs.append(st_spec)
        args.append(ctx[0])
    out_specs = [pl.BlockSpec((L, dv), lambda b, h: (b, h))]
    out_shape = [jax.ShapeDtypeStruct((rows, H * dv), BF16)]
    if emit_state:
        out_specs.append(st_spec)
        out_shape.append(jax.ShapeDtypeStruct((B, 2, H, dk, dv), F32))
    res = pl.pallas_call(
        functools.partial(_ret_kernel, nc=nc, dk=dk, has_s0=has_s0, emit_state=emit_state),
        grid=(B, H),
        in_specs=in_specs,
        out_specs=out_specs,
        out_shape=out_shape,
        scratch_shapes=[pltpu.VMEM((dk, dv), F32), pltpu.VMEM((dk, dv), F32), pltpu.VMEM((L, dv), F32)],
        compiler_params=_params("parallel", "parallel"),
        name="retention",
    )(*args)
    y = _outproj(res[0], p['out_w'].astype(BF16), x2, gate, rows_per_mod, *tiles['outproj'])
    return y, ((res[1],) if emit_state else None)


def _mlpre_kernel(xm_ref, cw_ref, cb_ref, wbd_ref, gw_ref, xc_ref, q_ref, k_ref, v_ref, gt_ref, *, bw, seq):
    R, tc = xm_ref.shape
    x = xm_ref[...].astype(F32)
    cw = cw_ref[...]
    W = cw.shape[0]
    pad = (W - 1) // 2
    pos = lax.broadcasted_iota(jnp.int32, (R, tc), 0) % seq
    acc = x * cw[pad:pad + 1] + cb_ref[...]
    for w in range(W):
        d = w - pad
        if d == 0:
            continue
        sh = pltpu.roll(x, (-d) % R, 0)
        valid = (pos < seq - d) if d > 0 else (pos >= -d)
        acc = acc + jnp.where(valid, sh, 0.0) * cw[w:w + 1]
    xc = _silu(acc)
    xc_ref[...] = xc.astype(BF16)
    xcb = xc.astype(BF16)
    xmb = xm_ref[...].astype(BF16)
    gsum = jnp.zeros(gt_ref.shape, F32)
    for blk in range(tc // bw):
        sl = slice(blk * bw, (blk + 1) * bw)
        q = jnp.dot(xcb[:, sl], wbd_ref[0, blk], preferred_element_type=F32).astype(BF16)
        k = jnp.dot(xcb[:, sl], wbd_ref[1, blk], preferred_element_type=F32).astype(BF16)
        v = jnp.dot(xmb[:, sl], wbd_ref[2, blk], preferred_element_type=F32).astype(BF16)
        q_ref[:, sl] = q
        k_ref[:, sl] = k
        v_ref[:, sl] = v
        gsum = gsum + jnp.dot(q, gw_ref[0, sl, :], preferred_element_type=F32)
        gsum = gsum + jnp.dot(k, gw_ref[1, sl, :], preferred_element_type=F32)
        gsum = gsum + jnp.dot(v, gw_ref[2, sl, :], preferred_element_type=F32)

    @pl.when(pl.program_id(1) == 0)
    def _():
        gt_ref[...] = gsum

    @pl.when(pl.program_id(1) != 0)
    def _():
        gt_ref[...] = gt_ref[...] + gsum


def _logsigmoid(x):
    return jnp.minimum(x, 0.0) - jnp.log1p(jnp.exp(-jnp.abs(x)))


def _mlscan_chunk(gl_ref, gb_ref, q_ref, k_ref, v_ref, h_ref, c_ref, n_ref, m_ref, d, dk, zero_state):
    C = CHUNK
    gl = gl_ref[0, 0, 0, 0] + gb_ref[0, 0]
    ih_row = gl[0:1]
    f_row = _logsigmoid(gl[1:2])
    ii = lax.broadcasted_iota(jnp.int32, (C, C), 0)
    jj = lax.broadcasted_iota(jnp.int32, (C, C), 1)
    sgn = 1 - 2 * d
    seen = (ii - jj) * sgn >= 0
    seen_t = (jj - ii) * sgn >= 0
    eye = ii == jj
    fmat = jnp.broadcast_to(f_row, (C, C))
    imat = jnp.broadcast_to(ih_row, (C, C))
    bh_col = jnp.sum(jnp.where(seen, fmat, 0.0), axis=1, keepdims=True)
    f_col = jnp.sum(jnp.where(eye, fmat, 0.0), axis=1, keepdims=True)
    ih_col = jnp.sum(jnp.where(eye, imat, 0.0), axis=1, keepdims=True)
    bh_row = jnp.sum(jnp.where(seen_t, jnp.broadcast_to(f_col, (C, C)), 0.0), axis=0, keepdims=True)
    tot = jnp.sum(f_row, axis=1, keepdims=True)
    m_prev = jnp.zeros((1, 1), F32) if zero_state else m_ref[...]
    dlog = jnp.where(seen, bh_col - bh_row + ih_row, -jnp.inf)
    inter = bh_col + m_prev
    m_i = jnp.maximum(jnp.max(dlog, axis=1, keepdims=True), inter)
    w = jnp.exp(dlog - m_i)
    wi = jnp.exp(inter - m_i)
    q = q_ref[...]
    kscale = dk ** -0.5
    s = lax.dot_general(q, k_ref[...], NT_DIMS, preferred_element_type=F32) * (w * kscale)
    sb = s.astype(BF16)
    den = jnp.sum(s, axis=1, keepdims=True)
    if not zero_state:
        den = den + wi * jnp.sum(q.astype(F32) * n_ref[...], axis=1, keepdims=True)
    inv_norm = 1.0 / jnp.maximum(jnp.abs(den), jnp.exp(-m_i))
    kw = tot - bh_col + ih_col
    m_new = jnp.maximum(tot + m_prev, jnp.max(kw, axis=0, keepdims=True))
    wk = jnp.exp(kw - m_new)
    dec = jnp.exp(tot + m_prev - m_new)
    kcw = k_ref[...].astype(F32) * (wk * kscale)
    kcb = kcw.astype(BF16)
    slab = _tile(dk, STATE_SLAB)
    for t in range(0, dk, slab):
        cols = slice(t, t + slab)
        vt = v_ref[:, cols]
        num = jnp.dot(sb, vt, preferred_element_type=F32)
        if not zero_state:
            num = num + wi * jnp.dot(q, c_ref[:, cols].astype(BF16), preferred_element_type=F32)
        h_ref[0, :, cols] = (num * inv_norm).astype(h_ref.dtype)
        kv = lax.dot_general(kcb, vt, TN_DIMS, preferred_element_type=F32)
        c_ref[:, cols] = kv if zero_state else dec * c_ref[:, cols] + kv
    ksum = jnp.sum(kcw, axis=0, keepdims=True)
    n_ref[...] = ksum if zero_state else dec * n_ref[...] + ksum
    m_ref[...] = m_new


def _mlscan_kernel(*refs, nblk, cps, dk, has_s0, emit_state):
    gl_ref, gb_ref, q_ref, k_ref, v_ref = refs[:5]
    i = 5
    if has_s0:
        c0_ref, n0_ref, m0_ref = refs[i:i + 3]
        i += 3
    h_ref = refs[i]
    i += 1
    if emit_state:
        c_ref, n_ref, m_ref = (r.at[0, 0, 0] for r in refs[i:i + 3])
    else:
        c_ref, n_ref, m_ref = refs[i:]
    d = pl.program_id(2)
    c = pl.program_id(3)
    static_zero_start = (not has_s0) and nblk == 1

    if not static_zero_start:
        @pl.when(c == 0)
        def _():
            if has_s0:
                c_ref[...] = c0_ref[0, 0, 0]
                n_ref[...] = n0_ref[0, 0, 0]
                m_ref[...] = m0_ref[0, 0, 0]
            else:
                c_ref[...] = jnp.zeros(c_ref.shape, F32)
                n_ref[...] = jnp.zeros(n_ref.shape, F32)
                m_ref[...] = jnp.zeros(m_ref.shape, F32)

    for step in range(cps):
        j = step + d * (cps - 1 - 2 * step)
        rows = pl.ds(pl.multiple_of(j * CHUNK, CHUNK), CHUNK)
        _mlscan_chunk(gl_ref.at[:, :, :, pl.ds(j, 1)], gb_ref, q_ref.at[rows, :], k_ref.at[rows, :],
                      v_ref.at[rows, :], h_ref.at[:, rows, :], c_ref, n_ref, m_ref, d, dk,
                      zero_state=static_zero_start and step == 0)


def _mlpost_kernel(h_ref, xc_ref, z_ref, gn_ref, sk_ref, o_ref):
    o = _rms(h_ref[0].astype(F32) + h_ref[1].astype(F32)) * gn_ref[...]
    o = (o + sk_ref[...] * xc_ref[...].astype(F32)) * _silu(z_ref[...].astype(F32))
    o_ref[...] = o.astype(BF16)


def _mlstm_layer(x2, B, L, p, mods, rows_per_mod, ctx, tiles):
    shift, scale, gate = mods
    H, dk = p['gn_g'].shape
    Wd = H * dk
    bsz = p['qkv_bd'].shape[-1]
    nc = L // CHUNK
    rows = x2.shape[0]
    has_s0 = ctx is not None
    emit_state = not has_s0
    u = _inproj(x2, p['norm_g'], scale, shift, p['in_w'].astype(BF16), rows_per_mod, BF16,
                *tiles['inproj'])

    bw = 256 if Wd % 256 == 0 else LANES
    nb = Wd // bw
    per = bw // bsz
    bd = p['qkv_bd'].reshape(3, nb, per, bsz, bsz)
    eye = jnp.eye(per, dtype=F32)
    wbd = jnp.einsum('tngio,gh->tngiho', bd, eye).reshape(3, nb, bw, bw).astype(BF16)
    n_gate = p['gate_w'].shape[-1]
    gw = p['gate_w'].reshape(2, 3, Wd, n_gate).transpose(1, 2, 0, 3).reshape(3, Wd, 2 * n_gate)
    gw = jnp.pad(gw, ((0, 0), (0, 0), (0, LANES - 2 * n_gate))).astype(BF16)

    tc = _tile(Wd, 512)
    nbt = tc // bw
    R = L * _tile(B, max(1, PRE_ROWS // L))
    xc, q, k, v, gates = pl.pallas_call(
        functools.partial(_mlpre_kernel, bw=bw, seq=L),
        grid=(rows // R, Wd // tc),
        in_specs=[pl.BlockSpec((R, tc), lambda b, j: (b, j)),
                  pl.BlockSpec((p['conv_w'].shape[0], tc), lambda b, j: (0, j)),
                  pl.BlockSpec((1, tc), lambda b, j: (0, j)),
                  pl.BlockSpec((3, nbt, bw, bw), lambda b, j: (0, j, 0, 0)),
                  pl.BlockSpec((3, tc, LANES), lambda b, j: (0, j, 0))],
        out_specs=[pl.BlockSpec((R, tc), lambda b, j: (b, j))] * 4
                  + [pl.BlockSpec((R, LANES), lambda b, j: (b, 0))],
        out_shape=[jax.ShapeDtypeStruct((rows, Wd), BF16)] * 4
                  + [jax.ShapeDtypeStruct((rows, LANES), F32)],
        compiler_params=_params("parallel", "arbitrary"),
        name="mlstm_pre",
    )(u, p['conv_w'], p['conv_b'].reshape(1, Wd), wbd, gw)

    gl = gates[:, :2 * n_gate].reshape(B, nc, CHUNK, 2, 2, H).transpose(0, 3, 5, 1, 4, 2)
    gb = p['gate_b'].reshape(2, 2, H).transpose(0, 2, 1).reshape(2, H, 2, 1)

    cps = _tile(nc, SCAN_CHUNKS_PER_STEP)
    nblk = nc // cps
    blk = cps * CHUNK

    def block_of(d, c):
        return c + d * (nblk - 1 - 2 * c)

    tok = lambda b, h, d, c: (b * nblk + block_of(d, c), h)
    in_specs = [pl.BlockSpec((1, 1, 1, cps, 2, CHUNK), lambda b, h, d, c: (b, d, h, block_of(d, c), 0, 0)),
                pl.BlockSpec((1, 1, 2, 1), lambda b, h, d, c: (d, h, 0, 0)),
                pl.BlockSpec((blk, dk), tok),
                pl.BlockSpec((blk, dk), tok),
                pl.BlockSpec((blk, dk), tok)]
    args = [gl, gb, q, k, v]
    c_spec = pl.BlockSpec((1, 1, 1, dk, dk), lambda b, h, d, c: (b, d, h, 0, 0))
    n_spec = pl.BlockSpec((1, 1, 1, 1, dk), lambda b, h, d, c: (b, d, h, 0, 0))
    m_spec = pl.BlockSpec((1, 1, 1, 1, 1), lambda b, h, d, c: (b, d, h, 0, 0))
    if has_s0:
        c0, n0, m0 = ctx
        in_specs += [c_spec, n_spec, m_spec]
        args += [c0, n0.reshape(B, 2, H, 1, dk), m0.reshape(B, 2, H, 1, 1)]
    out_specs = [pl.BlockSpec((1, blk, dk), lambda b, h, d, c: (d, b * nblk + block_of(d, c), h))]
    out_shape = [jax.ShapeDtypeStruct((2, rows, Wd), BF16)]
    if emit_state:
        out_specs += [c_spec, n_spec, m_spec]
        out_shape += [jax.ShapeDtypeStruct((B, 2, H, dk, dk), F32),
                      jax.ShapeDtypeStruct((B, 2, H, 1, dk), F32),
                      jax.ShapeDtypeStruct((B, 2, H, 1, 1), F32)]
    res = pl.pallas_call(
        functools.partial(_mlscan_kernel, nblk=nblk, cps=cps, dk=dk, has_s0=has_s0, emit_state=emit_state),
        grid=(B, H, 2, nblk),
        in_specs=in_specs,
        out_specs=out_specs,
        out_shape=out_shape,
        scratch_shapes=([] if emit_state else
                        [pltpu.VMEM((dk, dk), F32), pltpu.VMEM((1, dk), F32), pltpu.VMEM((1, 1), F32)]),
        compiler_params=_params("parallel", "parallel", "arbitrary", "arbitrary"),
        name="mlstm_scan",
    )(*args)
    hdir = res[0]

    tm = _tile(rows, 1024)
    o = pl.pallas_call(
        _mlpost_kernel,
        grid=(rows // tm, H),
        in_specs=[pl.BlockSpec((2, tm, dk), lambda i, h: (0, i, h)),
                  pl.BlockSpec((tm, dk), lambda i, h: (i, h)),
                  pl.BlockSpec((tm, dk), lambda i, h: (i, H + h)),
                  pl.BlockSpec((1, dk), lambda i, h: (0, h)),
                  pl.BlockSpec((1, dk), lambda i, h: (0, h))],
        out_specs=pl.BlockSpec((tm, dk), lambda i, h: (i, h)),
        out_shape=jax.ShapeDtypeStruct((rows, Wd), BF16),
        compiler_params=_params("parallel", "arbitrary"),
        name="mlstm_post",
    )(hdir, xc, u, p['gn_g'].reshape(1, Wd), p['skip'].reshape(1, Wd))
    y = _outproj(o, p['out_w'].astype(BF16), x2, gate, rows_per_mod, *tiles['outproj'])
    new = None
    if emit_state:
        new = (res[1], res[2].reshape(B, 2, H, dk), res[3].reshape(B, 2, H))
    return y, new


_BASE = dict(inproj=(1024, 2048), outproj=(512, 2048), tq=512)
_F32_OUT = dict(_BASE, inproj=(1024, 1024))
TILES_PROMPT = [_F32_OUT, _BASE, _BASE, _F32_OUT]
TILES_SAMPLE = [_BASE, _BASE, _BASE, _BASE]

def kernel(x_prompt, x_sample, c, c_ctx, cache_k_l0, cache_v_l0, state_ret_l1, state_C_l2, state_n_l2, state_m_l2, cache_k_l3, cache_v_l3, norm_g_l0, ada_w_l0, ada_b_l0, in_w_l0, out_w_l0, qk_g_l0, lam_l0, subln_g_l0, norm_g_l1, ada_w_l1, ada_b_l1, in_w_l1, out_w_l1, decay_l1, gn_g_l1, norm_g_l2, ada_w_l2, ada_b_l2, in_w_l2, out_w_l2, conv_w_l2, conv_b_l2, qkv_bd_l2, gate_w_l2, gate_b_l2, gn_g_l2, skip_l2, norm_g_l3, ada_w_l3, ada_b_l3, in_w_l3, out_w_l3, qk_g_l3, lam_l3, subln_g_l3):
    layers = [
        dict(norm_g=norm_g_l0, ada_w=ada_w_l0, ada_b=ada_b_l0, in_w=in_w_l0, out_w=out_w_l0,
             qk_g=qk_g_l0, lam=lam_l0, subln_g=subln_g_l0),
        dict(norm_g=norm_g_l1, ada_w=ada_w_l1, ada_b=ada_b_l1, in_w=in_w_l1, out_w=out_w_l1,
             decay=decay_l1, gn_g=gn_g_l1),
        dict(norm_g=norm_g_l2, ada_w=ada_w_l2, ada_b=ada_b_l2, in_w=in_w_l2, out_w=out_w_l2,
             conv_w=conv_w_l2, conv_b=conv_b_l2, qkv_bd=qkv_bd_l2, gate_w=gate_w_l2, gate_b=gate_b_l2,
             gn_g=gn_g_l2, skip=skip_l2),
        dict(norm_g=norm_g_l3, ada_w=ada_w_l3, ada_b=ada_b_l3, in_w=in_w_l3, out_w=out_w_l3,
             qk_g=qk_g_l3, lam=lam_l3, subln_g=subln_g_l3),
    ]
    ctxs = [(cache_k_l0, cache_v_l0), (state_ret_l1,), (state_C_l2, state_n_l2, state_m_l2),
            (cache_k_l3, cache_v_l3)]
    Bp, Lp, D = x_prompt.shape
    Bs, Ls, _ = x_sample.shape
    xp = x_prompt.reshape(Bp * Lp, D)
    xs = x_sample.reshape(Bs * Ls, D)
    n_cond = 1 + Bs
    cond_rows = -(-n_cond // 8) * 8
    cond = jnp.concatenate([c_ctx[None, :], c, jnp.zeros((cond_rows - n_cond, D), F32)], axis=0)
    new_states = []
    for i, p in enumerate(layers):
        m = _adaln(cond, p['ada_w'], p['ada_b'])
        mods_p = tuple(m[0:1, k * D:(k + 1) * D].reshape(1, 1, D) for k in range(3))
        mods_s = tuple(m[1:n_cond, k * D:(k + 1) * D].reshape(Bs, 1, D) for k in range(3))
        kind = i % 3
        tp, ts = TILES_PROMPT[i], TILES_SAMPLE[i]
        if kind == 0:
            xp, st = _diff_attention_layer(xp, Bp, Lp, p, mods_p, Bp * Lp, i, None, tp)
            xs, _ = _diff_attention_layer(xs, Bs, Ls, p, mods_s, Ls, i, ctxs[i], ts)
        elif kind == 1:
            xp, st = _retention_layer(xp, Bp, Lp, p, mods_p, Bp * Lp, None, tp)
            xs, _ = _retention_layer(xs, Bs, Ls, p, mods_s, Ls, ctxs[i], ts)
        else:
            xp, st = _mlstm_layer(xp, Bp, Lp, p, mods_p, Bp * Lp, None, tp)
            xs, _ = _mlstm_layer(xs, Bs, Ls, p, mods_s, Ls, ctxs[i], ts)
        new_states.append(st)
    (k_l0, v_l0), (ret_l1,), (C_l2, n_l2, m_l2), (k_l3, v_l3) = new_states
    return (xp.reshape(Bp, Lp, D), xs.reshape(Bs, Ls, D), k_l0, v_l0, ret_l1, C_l2, n_l2, m_l2, k_l3, v_l3)
```

```python
import functools
import math

import jax
import jax.numpy as jnp
from jax import lax
from jax.experimental import pallas as pl
from jax.experimental.pallas import tpu as pltpu

F32 = jnp.float32
BF16 = jnp.bfloat16

CHUNK = 256
GRID_W = 64
ROPE_BASE = 10000.0
NORM_EPS = 1e-6
LANES = 128
VMEM_LIMIT_BYTES = 56 * 1024 * 1024

LOG2E = math.log2(math.e)
MAX_UNSHIFTED_SCORE = 60.0
KEY_BLOCK = 512
PRE_ROWS = 2048
STATE_SLAB = 256
SCAN_CHUNKS_PER_STEP = 4

NT_DIMS = (((1,), (1,)), ((), ()))
TN_DIMS = (((0,), (0,)), ((), ()))


def _tile(n, pref):
    if n <= pref:
        return n
    t = pref
    while n % t:
        t -= 1
    return t


def _params(*sem):
    return pltpu.CompilerParams(dimension_semantics=sem, vmem_limit_bytes=VMEM_LIMIT_BYTES)


def _silu(x):
    return x * jax.nn.sigmoid(x)


def _rms(x):
    return x * lax.rsqrt(jnp.mean(x * x, axis=-1, keepdims=True) + NORM_EPS)


def _adaln_kernel(c_ref, w_ref, b_ref, o_ref):
    s = _silu(c_ref[...]).astype(BF16)
    o_ref[...] = jnp.dot(s, w_ref[...].astype(BF16), preferred_element_type=F32) + b_ref[...]


def _adaln(cond, w, b):
    R, D = cond.shape
    N = w.shape[1]
    tn = _tile(N, 512)
    return pl.pallas_call(
        _adaln_kernel,
        grid=(N // tn,),
        in_specs=[pl.BlockSpec((R, D), lambda j: (0, 0)),
                  pl.BlockSpec((D, tn), lambda j: (0, j)),
                  pl.BlockSpec((1, tn), lambda j: (0, j))],
        out_specs=pl.BlockSpec((R, tn), lambda j: (0, j)),
        out_shape=jax.ShapeDtypeStruct((R, N), F32),
        compiler_params=_params("arbitrary"),
        name="adaln",
    )(cond, w, b.reshape(1, N))


def _inproj_kernel(x_ref, g_ref, sc_ref, sh_ref, w_ref, o_ref, h_ref):
    @pl.when(pl.program_id(1) == 0)
    def _():
        gain = g_ref[...] * (1.0 + sc_ref[0])
        h_ref[...] = (_rms(x_ref[...]) * gain + sh_ref[0]).astype(BF16)

    o_ref[...] = jnp.dot(h_ref[...], w_ref[...], preferred_element_type=F32).astype(o_ref.dtype)


def _inproj(x2, norm_g, scale, shift, w, rows_per_mod, out_dtype, tm_pref=1024, tn_pref=512):
    rows, D = x2.shape
    N = w.shape[1]
    tm = _tile(rows_per_mod, tm_pref)
    tn = _tile(N, tn_pref)
    mod = lambda i, j: ((i * tm) // rows_per_mod, 0, 0)
    return pl.pallas_call(
        _inproj_kernel,
        grid=(rows // tm, N // tn),
        in_specs=[pl.BlockSpec((tm, D), lambda i, j: (i, 0)),
                  pl.BlockSpec((1, D), lambda i, j: (0, 0)),
                  pl.BlockSpec((1, 1, D), mod),
                  pl.BlockSpec((1, 1, D), mod),
                  pl.BlockSpec((D, tn), lambda i, j: (0, j))],
        out_specs=pl.BlockSpec((tm, tn), lambda i, j: (i, j)),
        out_shape=jax.ShapeDtypeStruct((rows, N), out_dtype),
        scratch_shapes=[pltpu.VMEM((tm, D), BF16)],
        compiler_params=_params("parallel", "arbitrary"),
        name="inproj",
    )(x2, norm_g.reshape(1, D), scale, shift, w)


def _outproj_kernel(a_ref, w_ref, x_ref, gt_ref, o_ref):
    y = jnp.dot(a_ref[...], w_ref[...], preferred_element_type=F32)
    o_ref[...] = x_ref[...] + gt_ref[0] * y


def _outproj(a, w, x2, gate, rows_per_mod, tm_pref=1024, tn_pref=512):
    rows, K = a.shape
    D = w.shape[1]
    tm = _tile(rows_per_mod, tm_pref)
    tn = _tile(D, tn_pref)
    w_mode = dict(pipeline_mode=pl.Buffered(1)) if tn == D else {}
    return pl.pallas_call(
        _outproj_kernel,
        grid=(rows // tm, D // tn),
        in_specs=[pl.BlockSpec((tm, K), lambda i, j: (i, 0)),
                  pl.BlockSpec((K, tn), lambda i, j: (0, j), **w_mode),
                  pl.BlockSpec((tm, tn), lambda i, j: (i, j)),
                  pl.BlockSpec((1, 1, tn), lambda i, j: ((i * tm) // rows_per_mod, 0, j))],
        out_specs=pl.BlockSpec((tm, tn), lambda i, j: (i, j)),
        out_shape=jax.ShapeDtypeStruct((rows, D), F32),
        compiler_params=_params("parallel", "arbitrary"),
        name="outproj",
    )(a, w, x2, gate)


def _rope_tables(L, dh):
    t = jnp.arange(L)
    row = (t // GRID_W).astype(F32)
    col = (t % GRID_W).astype(F32)
    n_freq = dh // 4
    inv = ROPE_BASE ** (-jnp.arange(n_freq, dtype=F32) / n_freq)
    ar = row[:, None] * inv[None, :]
    ac = col[:, None] * inv[None, :]
    cos = jnp.concatenate([jnp.cos(ar), jnp.cos(ar), jnp.cos(ac), jnp.cos(ac)], axis=-1)
    sin = jnp.concatenate([-jnp.sin(ar), jnp.sin(ar), -jnp.sin(ac), jnp.sin(ac)], axis=-1)
    return cos.astype(F32), sin.astype(F32)


def _rope_gain_tables(L, dh, qk_g):
    cos, sin = _rope_tables(L, dh)
    scale = jnp.array([dh ** -0.5 * LOG2E, 1.0], F32)[:, None]
    g = qk_g.astype(F32) * scale
    g_swapped = g.reshape(2, 2, 2, dh // 4)[:, :, ::-1, :].reshape(2, dh)
    return g[:, None, :] * cos[None], g_swapped[:, None, :] * sin[None]


def _qkpost_kernel(u_ref, uv_ref, g_ref, o_ref, kf_ref, vf_ref, *, n_sub, dh, n_head, dv):
    g = g_ref[0]
    j = pl.program_id(1)
    factor = jnp.where(j == 0, dh ** -0.5 * LOG2E, 1.0)
    for s in range(n_sub):
        sl = slice(s * dh, (s + 1) * dh)
        y = _rms(u_ref[:, sl].astype(F32)) * g
        o_ref[:, sl] = (y * factor).astype(BF16)
        kf_ref[:, s, :] = y

    @pl.when(j == 0)
    def _():
        for h in range(n_head):
            vf_ref[:, h, :] = uv_ref[:, h * dv:(h + 1) * dv]


def _qkpost(u, qk_g, L, n_sub, dh, n_head, dv):
    rows = u.shape[0]
    W = n_sub * dh
    Wv = n_head * dv
    tm = _tile(L, 512)
    return pl.pallas_call(
        functools.partial(_qkpost_kernel, n_sub=n_sub, dh=dh, n_head=n_head, dv=dv),
        grid=(rows // tm, 2),
        in_specs=[pl.BlockSpec((tm, W), lambda i, j: (i, j)),
                  pl.BlockSpec((tm, Wv), lambda i, j: (i, 2 * W // Wv)),
                  pl.BlockSpec((1, 1, dh), lambda i, j: (j, 0, 0))],
        out_specs=[pl.BlockSpec((tm, W), lambda i, j: (i, j)),
                   pl.BlockSpec((tm, n_sub, dh), lambda i, j: (i, 0, 0)),
                   pl.BlockSpec((tm, n_head, dv), lambda i, j: (i, 0, 0))],
        out_shape=[jax.ShapeDtypeStruct((rows, 2 * W), BF16),
                   jax.ShapeDtypeStruct((rows, n_sub, dh), F32),
                   jax.ShapeDtypeStruct((rows, n_head, dv), F32)],
        compiler_params=_params("parallel", "arbitrary"),
        name="qkpost",
    )(u, u, qk_g.reshape(2, 1, dh))


def _qkrope_kernel(u_ref, t1_ref, t2_ref, o_ref, *, n_sub, dh):
    t1 = t1_ref[0]
    t2 = t2_ref[0]
    lane = lax.broadcasted_iota(jnp.int32, t1.shape, 1)
    first = (lane % (dh // 2)) < (dh // 4)
    ones = jnp.ones((dh, dh), BF16)
    for s in range(n_sub):
        sl = slice(s * dh, (s + 1) * dh)
        x = u_ref[:, sl].astype(F32)
        sq = x * x
        hi = sq.astype(BF16)
        lo = (sq - hi.astype(F32)).astype(BF16)
        ssq = (jnp.dot(hi, ones, preferred_element_type=F32)
               + jnp.dot(lo, ones, preferred_element_type=F32))
        r = lax.rsqrt(ssq * (1.0 / dh) + NORM_EPS)
        swapped = jnp.where(first, pltpu.roll(x, dh - dh // 4, 1), pltpu.roll(x, dh // 4, 1))
        o_ref[:, sl] = ((x * t1 + swapped * t2) * r).astype(BF16)


def _qkrope(u, qk_g, L, n_sub, dh):
    rows = u.shape[0]
    W = n_sub * dh
    tm = _tile(L, 512)
    nl = L // tm
    t1, t2 = _rope_gain_tables(L, dh, qk_g)
    tab = pl.BlockSpec((1, tm, dh), lambda i, j: (j, i % nl, 0))
    return pl.pallas_call(
        functools.partial(_qkrope_kernel, n_sub=n_sub, dh=dh),
        grid=(rows // tm, 2),
        in_specs=[pl.BlockSpec((tm, W), lambda i, j: (i, j)), tab, tab],
        out_specs=pl.BlockSpec((tm, W), lambda i, j: (i, j)),
        out_shape=jax.ShapeDtypeStruct((rows, 2 * W), BF16),
        compiler_params=_params("parallel", "arbitrary"),
        name="qkrope",
    )(u, t1, t2)


def _attn_kernel(*refs, dh, lam_init, has_ctx, stable):
    if has_ctx:
        lam_ref, sg_ref, q_ref, k_ref, v_ref, g_ref, ck_ref, cv_ref, o_ref = refs
    else:
        lam_ref, sg_ref, q_ref, k_ref, v_ref, g_ref, o_ref = refs
    lm = lam_ref[...]
    lam = (jnp.exp(jnp.sum(lm[0:1] * lm[1:2], axis=1, keepdims=True))
           - jnp.exp(jnp.sum(lm[2:3] * lm[3:4], axis=1, keepdims=True)) + lam_init)
    L = k_ref.shape[0]
    o = None
    for c in range(2):
        sl = slice(c * dh, (c + 1) * dh)
        qc = q_ref[:, sl]
        if stable:
            v = v_ref[...].astype(BF16)
            s1 = lax.dot_general(qc, k_ref[:, sl], NT_DIMS, preferred_element_type=F32)
            m = jnp.max(s1, axis=-1, keepdims=True)
            if has_ctx:
                s2 = lax.dot_general(qc, ck_ref[0, :, sl], NT_DIMS, preferred_element_type=F32)
                m = jnp.maximum(m, jnp.max(s2, axis=-1, keepdims=True))
            p1 = jnp.exp2(s1 - m)
            l = jnp.sum(p1, axis=-1, keepdims=True)
            oc = jnp.dot(p1.astype(BF16), v, preferred_element_type=F32)
            if has_ctx:
                p2 = jnp.exp2(s2 - m)
                l = l + jnp.sum(p2, axis=-1, keepdims=True)
                oc = oc + jnp.dot(p2.astype(BF16), cv_ref[0], preferred_element_type=F32)
        else:
            kb = _tile(L, KEY_BLOCK)
            blocks = [(k_ref[s:s + kb, sl], v_ref[s:s + kb, :]) for s in range(0, L, kb)]
            if has_ctx:
                P = ck_ref.shape[1]
                kb = _tile(P, KEY_BLOCK)
                blocks += [(ck_ref[0, s:s + kb, sl], cv_ref[0, s:s + kb, :]) for s in range(0, P, kb)]
            lanes = None
            oc = None
            for kb, vb in blocks:
                p = jnp.exp2(lax.dot_general(qc, kb, NT_DIMS, preferred_element_type=F32))
                for t in range(0, p.shape[1], LANES):
                    lanes = p[:, t:t + LANES] if lanes is None else lanes + p[:, t:t + LANES]
                ob = jnp.dot(p.astype(BF16), vb.astype(BF16), preferred_element_type=F32)
                oc = ob if oc is None else oc + ob
            l = jnp.sum(lanes, axis=-1, keepdims=True)
        coef = (1.0 / l) if c == 0 else (-lam / l)
        o = oc * coef if o is None else o + oc * coef
    o = _rms(o) * sg_ref[...] * (1.0 - lam_init)
    o_ref[...] = (o * _silu(g_ref[...].astype(F32))).astype(BF16)


def _attention(qk, u, lam_p, subln_g, ctx, B, L, H, dh, lam_init, tq_pref, stable):
    rows = qk.shape[0]
    V = 2 * dh
    tq = _tile(L, tq_pref)
    nq = L // tq
    has_ctx = ctx is not None
    in_specs = [pl.BlockSpec((4, dh), lambda b, h, i: (0, 0)),
                pl.BlockSpec((1, V), lambda b, h, i: (0, 0)),
                pl.BlockSpec((tq, V), lambda b, h, i: (b * nq + i, h)),
                pl.BlockSpec((L, V), lambda b, h, i: (b, H + h)),
                pl.BlockSpec((L, V), lambda b, h, i: (b, 2 * H + h)),
                pl.BlockSpec((tq, V), lambda b, h, i: (b * nq + i, 3 * H + h))]
    args = [lam_p, subln_g.reshape(1, V), qk, qk, u, u]
    if has_ctx:
        P = ctx[0].shape[1]
        in_specs += [pl.BlockSpec((1, P, V), lambda b, h, i: (b, 0, h))] * 2
        args += list(ctx)
    return pl.pallas_call(
        functools.partial(_attn_kernel, dh=dh, lam_init=lam_init, has_ctx=has_ctx, stable=stable),
        grid=(B, H, nq),
        in_specs=in_specs,
        out_specs=pl.BlockSpec((tq, V), lambda b, h, i: (b * nq + i, h)),
        out_shape=jax.ShapeDtypeStruct((rows, H * V), BF16),
        compiler_params=_params("parallel", "parallel", "arbitrary"),
        name="diffattn_stable" if stable else "diffattn",
    )(*args)


def _score_bound(qk_g, dh, cache_k):
    qn = math.sqrt(dh) * jnp.max(jnp.abs(qk_g[0])) * (dh ** -0.5 * LOG2E)
    kn = math.sqrt(dh) * jnp.max(jnp.abs(qk_g[1]))
    if cache_k is not None:
        kn = jnp.maximum(kn, jnp.sqrt(jnp.max(jnp.sum(jnp.square(cache_k.astype(F32)), axis=-1))))
    return 1.01 * qn * kn


def _diff_attention_layer(x2, B, L, p, mods, rows_per_mod, layer_idx, ctx, tiles):
    shift, scale, gate = mods
    dh = p['qk_g'].shape[1]
    V = p['subln_g'].shape[0]
    QKW = (p['in_w'].shape[1] - 2 * p['out_w'].shape[0]) // 2
    H = p['out_w'].shape[0] // V
    n_sub = QKW // dh
    lam_init = 0.8 - 0.6 * math.exp(-0.3 * layer_idx)
    is_prompt = ctx is None
    new = None
    if is_prompt:
        u = _inproj(x2, p['norm_g'], scale, shift, p['in_w'].astype(BF16), rows_per_mod, F32,
                    *tiles['inproj'])
        qk, kf, vf = _qkpost(u, p['qk_g'], L, n_sub, dh, H, V)
        new = (kf.reshape(B, L, n_sub, dh), vf.reshape(B, L, H, V))
        cache = None
    else:
        u = _inproj(x2, p['norm_g'], scale, shift, p['in_w'].astype(BF16), rows_per_mod, BF16,
                    *tiles['inproj'])
        qk = _qkrope(u, p['qk_g'], L, n_sub, dh)
        ck, cv = ctx
        P = ck.shape[1]
        cache = (ck.reshape(B, P, QKW).astype(BF16), cv.reshape(B, P, H * V).astype(BF16))
    def attend(stable):
        return _attention(qk, u, p['lam'], p['subln_g'], cache, B, L, H, dh, lam_init, tiles['tq'], stable)

    bound = _score_bound(p['qk_g'], dh, None if is_prompt else ctx[0])
    o = lax.cond(bound <= MAX_UNSHIFTED_SCORE, lambda: attend(False), lambda: attend(True))
    y = _outproj(o, p['out_w'].astype(BF16), x2, gate, rows_per_mod, *tiles['outproj'])
    return y, new


def _ret_kernel(*refs, nc, dk, has_s0, emit_state):
    dec_ref, gn_ref, q_ref, k_ref, v_ref, g_ref = refs[:6]
    i = 6
    if has_s0:
        s0_ref = refs[i]
        i += 1
    o_ref = refs[i]
    i += 1
    if emit_state:
        so_ref = refs[i]
        i += 1
    sf_ref, sb_ref, acc_ref = refs[i:]
    C = CHUNK
    lg = jnp.log1p(-jnp.exp(dec_ref[0]))
    lgf = lg[0:1]
    lgb = lg[1:2]
    ii = lax.broadcasted_iota(jnp.int32, (C, C), 0)
    jj = lax.broadcasted_iota(jnp.int32, (C, C), 1)
    rel = (ii - jj).astype(F32)
    dtot = (jnp.where(rel >= 0, jnp.exp(jnp.maximum(rel, 0.0) * lgf), 0.0)
            + jnp.where(rel <= 0, jnp.exp(jnp.maximum(-rel, 0.0) * lgb), 0.0))
    pos = lax.broadcasted_iota(jnp.int32, (C, 1), 0).astype(F32)
    qdf = jnp.exp((pos + 1.0) * lgf)
    kdf = jnp.exp((C - 1.0 - pos) * lgf)
    qdb = jnp.exp((C - pos) * lgb)
    kdb = jnp.exp(pos * lgb)
    cdf = jnp.exp(C * lgf)
    cdb = jnp.exp(C * lgb)
    scale = dk ** -0.5
    zero_start = (not has_s0) and nc == 1
    if has_s0:
        sf_ref[...] = s0_ref[0, 0, 0]
        sb_ref[...] = s0_ref[0, 1, 0]
    elif not zero_start:
        sf_ref[...] = jnp.zeros_like(sf_ref)
        sb_ref[...] = jnp.zeros_like(sb_ref)
    gn = gn_ref[0]

    def load(c):
        r = pl.multiple_of(c * C, C)
        q = q_ref[pl.ds(r, C), :]
        k = k_ref[pl.ds(r, C), :].astype(F32) * scale
        v = v_ref[pl.ds(r, C), :].astype(BF16)
        return r, q, k, v

    def fwd(c, carry):
        r, q, k, v = load(c)
        att = lax.dot_general(q, k.astype(BF16), NT_DIMS, preferred_element_type=F32) * dtot
        o = jnp.dot(att.astype(BF16), v, preferred_element_type=F32)
        kv = lax.dot_general((k * kdf).astype(BF16), v, TN_DIMS, preferred_element_type=F32)
        if zero_start:
            sf_ref[...] = kv
        else:
            o = o + jnp.dot((q.astype(F32) * qdf).astype(BF16), sf_ref[...].astype(BF16),
                            preferred_element_type=F32)
            sf_ref[...] = cdf * sf_ref[...] + kv
        acc_ref[pl.ds(r, C), :] = o
        return carry

    lax.fori_loop(0, nc, fwd, 0, unroll=min(nc, 8))

    def bwd(t, carry):
        r, q, k, v = load(nc - 1 - t)
        o = acc_ref[pl.ds(r, C), :]
        kv = lax.dot_general((k * kdb).astype(BF16), v, TN_DIMS, preferred_element_type=F32)
        if zero_start:
            sb_ref[...] = kv
        else:
            o = o + jnp.dot((q.astype(F32) * qdb).astype(BF16), sb_ref[...].astype(BF16),
                            preferred_element_type=F32)
            sb_ref[...] = cdb * sb_ref[...] + kv
        g = g_ref[pl.ds(r, C), :].astype(F32)
        o_ref[pl.ds(r, C), :] = (_rms(o) * gn * _silu(g)).astype(BF16)
        return carry

    lax.fori_loop(0, nc, bwd, 0, unroll=min(nc, 8))
    if emit_state:
        so_ref[0, 0, 0] = sf_ref[...]
        so_ref[0, 1, 0] = sb_ref[...]


def _retention_layer(x2, B, L, p, mods, rows_per_mod, ctx, tiles):
    shift, scale, gate = mods
    H, dv = p['gn_g'].shape
    dk = (p['in_w'].shape[1] - 2 * H * dv) // (2 * H)
    nc = L // CHUNK
    rows = x2.shape[0]
    has_s0 = ctx is not None
    emit_state = not has_s0
    u = _inproj(x2, p['norm_g'], scale, shift, p['in_w'].astype(BF16), rows_per_mod, BF16,
                *tiles['inproj'])
    kq = (H * dk) // dk
    kv = (2 * H * dk) // dv
    in_specs = [pl.BlockSpec((1, 2, 1), lambda b, h: (h, 0, 0)),
                pl.BlockSpec((1, 1, dv), lambda b, h: (h, 0, 0)),
                pl.BlockSpec((L, dk), lambda b, h: (b, h)),
                pl.BlockSpec((L, dk), lambda b, h: (b, kq + h)),
                pl.BlockSpec((L, dv), lambda b, h: (b, kv + h)),
                pl.BlockSpec((L, dv), lambda b, h: (b, kv + H + h))]
    args = [p['decay'].T.reshape(H, 2, 1), p['gn_g'].reshape(H, 1, dv), u, u, u, u]
    st_spec = pl.BlockSpec((1, 2, 1, dk, dv), lambda b, h: (b, 0, h, 0, 0))
    if has_s0:
        in_specs.append(st_spec)
        args.append(ctx[0])
    out_specs = [pl.BlockSpec((L, dv), lambda b, h: (b, h))]
    out_shape = [jax.ShapeDtypeStruct((rows, H * dv), BF16)]
    if emit_state:
        out_specs.append(st_spec)
        out_shape.append(jax.ShapeDtypeStruct((B, 2, H, dk, dv), F32))
    res = pl.pallas_call(
        functools.partial(_ret_kernel, nc=nc, dk=dk, has_s0=has_s0, emit_state=emit_state),
        grid=(B, H),
        in_specs=in_specs,
        out_specs=out_specs,
        out_shape=out_shape,
        scratch_shapes=[pltpu.VMEM((dk, dv), F32), pltpu.VMEM((dk, dv), F32), pltpu.VMEM((L, dv), F32)],
        compiler_params=_params("parallel", "parallel"),
        name="retention",
    )(*args)
    y = _outproj(res[0], p['out_w'].astype(BF16), x2, gate, rows_per_mod, *tiles['outproj'])
    return y, ((res[1],) if emit_state else None)


def _mlpre_kernel(xm_ref, cw_ref, cb_ref, wbd_ref, gw_ref, xc_ref, q_ref, k_ref, v_ref, gt_ref, *, bw, seq):
    R, tc = xm_ref.shape
    x = xm_ref[...].astype(F32)
    cw = cw_ref[...]
    W = cw.shape[0]
    pad = (W - 1) // 2
    pos = lax.broadcasted_iota(jnp.int32, (R, tc), 0) % seq
    acc = x * cw[pad:pad + 1] + cb_ref[...]
    for w in range(W):
        d = w - pad
        if d == 0:
            continue
        sh = pltpu.roll(x, (-d) % R, 0)
        valid = (pos < seq - d) if d > 0 else (pos >= -d)
        acc = acc + jnp.where(valid, sh, 0.0) * cw[w:w + 1]
    xc = _silu(acc)
    xc_ref[...] = xc.astype(BF16)
    xcb = xc.astype(BF16)
    xmb = xm_ref[...].astype(BF16)
    gsum = jnp.zeros(gt_ref.shape, F32)
    for blk in range(tc // bw):
        sl = slice(blk * bw, (blk + 1) * bw)
        q = jnp.dot(xcb[:, sl], wbd_ref[0, blk], preferred_element_type=F32).astype(BF16)
        k = jnp.dot(xcb[:, sl], wbd_ref[1, blk], preferred_element_type=F32).astype(BF16)
        v = jnp.dot(xmb[:, sl], wbd_ref[2, blk], preferred_element_type=F32).astype(BF16)
        q_ref[:, sl] = q
        k_ref[:, sl] = k
        v_ref[:, sl] = v
        gsum = gsum + jnp.dot(q, gw_ref[0, sl, :], preferred_element_type=F32)
        gsum = gsum + jnp.dot(k, gw_ref[1, sl, :], preferred_element_type=F32)
        gsum = gsum + jnp.dot(v, gw_ref[2, sl, :], preferred_element_type=F32)

    @pl.when(pl.program_id(1) == 0)
    def _():
        gt_ref[...] = gsum

    @pl.when(pl.program_id(1) != 0)
    def _():
        gt_ref[...] = gt_ref[...] + gsum


def _logsigmoid(x):
    return jnp.minimum(x, 0.0) - jnp.log1p(jnp.exp(-jnp.abs(x)))


def _mlscan_chunk(gl_ref, gb_ref, q_ref, k_ref, v_ref, h_ref, c_ref, n_ref, m_ref, d, dk, zero_state):
    C = CHUNK
    gl = gl_ref[0, 0, 0, 0] + gb_ref[0, 0]
    ih_row = gl[0:1]
    f_row = _logsigmoid(gl[1:2])
    ii = lax.broadcasted_iota(jnp.int32, (C, C), 0)
    jj = lax.broadcasted_iota(jnp.int32, (C, C), 1)
    sgn = 1 - 2 * d
    seen = (ii - jj) * sgn >= 0
    seen_t = (jj - ii) * sgn >= 0
    eye = ii == jj
    fmat = jnp.broadcast_to(f_row, (C, C))
    imat = jnp.broadcast_to(ih_row, (C, C))
    bh_col = jnp.sum(jnp.where(seen, fmat, 0.0), axis=1, keepdims=True)
    f_col = jnp.sum(jnp.where(eye, fmat, 0.0), axis=1, keepdims=True)
    ih_col = jnp.sum(jnp.where(eye, imat, 0.0), axis=1, keepdims=True)
    bh_row = jnp.sum(jnp.where(seen_t, jnp.broadcast_to(f_col, (C, C)), 0.0), axis=0, keepdims=True)
    tot = jnp.sum(f_row, axis=1, keepdims=True)
    m_prev = jnp.zeros((1, 1), F32) if zero_state else m_ref[...]
    dlog = jnp.where(seen, bh_col - bh_row + ih_row, -jnp.inf)
    inter = bh_col + m_prev
    m_i = jnp.maximum(jnp.max(dlog, axis=1, keepdims=True), inter)
    w = jnp.exp(dlog - m_i)
    wi = jnp.exp(inter - m_i)
    q = q_ref[...]
    kscale = dk ** -0.5
    s = lax.dot_general(q, k_ref[...], NT_DIMS, preferred_element_type=F32) * (w * kscale)
    sb = s.astype(BF16)
    den = jnp.sum(s, axis=1, keepdims=True)
    if not zero_state:
        den = den + wi * jnp.sum(q.astype(F32) * n_ref[...], axis=1, keepdims=True)
    inv_norm = 1.0 / jnp.maximum(jnp.abs(den), jnp.exp(-m_i))
    kw = tot - bh_col + ih_col
    m_new = jnp.maximum(tot + m_prev, jnp.max(kw, axis=0, keepdims=True))
    wk = jnp.exp(kw - m_new)
    dec = jnp.exp(tot + m_prev - m_new)
    kcw = k_ref[...].astype(F32) * (wk * kscale)
    kcb = kcw.astype(BF16)
    slab = _tile(dk, STATE_SLAB)
    for t in range(0, dk, slab):
        cols = slice(t, t + slab)
        vt = v_ref[:, cols]
        num = jnp.dot(sb, vt, preferred_element_type=F32)
        if not zero_state:
            num = num + wi * jnp.dot(q, c_ref[:, cols].astype(BF16), preferred_element_type=F32)
        h_ref[0, :, cols] = (num * inv_norm).astype(h_ref.dtype)
        kv = lax.dot_general(kcb, vt, TN_DIMS, preferred_element_type=F32)
        c_ref[:, cols] = kv if zero_state else dec * c_ref[:, cols] + kv
    ksum = jnp.sum(kcw, axis=0, keepdims=True)
    n_ref[...] = ksum if zero_state else dec * n_ref[...] + ksum
    m_ref[...] = m_new


def _mlscan_kernel(*refs, nblk, cps, dk, has_s0, emit_state):
    gl_ref, gb_ref, q_ref, k_ref, v_ref = refs[:5]
    i = 5
    if has_s0:
        c0_ref, n0_ref, m0_ref = refs[i:i + 3]
        i += 3
    h_ref = refs[i]
    i += 1
    if emit_state:
        c_ref, n_ref, m_ref = (r.at[0, 0, 0] for r in refs[i:i + 3])
    else:
        c_ref, n_ref, m_ref = refs[i:]
    d = pl.program_id(2)
    c = pl.program_id(3)
    static_zero_start = (not has_s0) and nblk == 1

    if not static_zero_start:
        @pl.when(c == 0)
        def _():
            if has_s0:
                c_ref[...] = c0_ref[0, 0, 0]
                n_ref[...] = n0_ref[0, 0, 0]
                m_ref[...] = m0_ref[0, 0, 0]
            else:
                c_ref[...] = jnp.zeros(c_ref.shape, F32)
                n_ref[...] = jnp.zeros(n_ref.shape, F32)
                m_ref[...] = jnp.zeros(m_ref.shape, F32)

    for step in range(cps):
        j = step + d * (cps - 1 - 2 * step)
        rows = pl.ds(pl.multiple_of(j * CHUNK, CHUNK), CHUNK)
        _mlscan_chunk(gl_ref.at[:, :, :, pl.ds(j, 1)], gb_ref, q_ref.at[rows, :], k_ref.at[rows, :],
                      v_ref.at[rows, :], h_ref.at[:, rows, :], c_ref, n_ref, m_ref, d, dk,
                      zero_state=static_zero_start and step == 0)


def _mlpost_kernel(h_ref, xc_ref, z_ref, gn_ref, sk_ref, o_ref):
    o = _rms(h_ref[0].astype(F32) + h_ref[1].astype(F32)) * gn_ref[...]
    o = (o + sk_ref[...] * xc_ref[...].astype(F32)) * _silu(z_ref[...].astype(F32))
    o_ref[...] = o.astype(BF16)


def _mlstm_layer(x2, B, L, p, mods, rows_per_mod, ctx, tiles):
    shift, scale, gate = mods
    H, dk = p['gn_g'].shape
    Wd = H * dk
    bsz = p['qkv_bd'].shape[-1]
    nc = L // CHUNK
    rows = x2.shape[0]
    has_s0 = ctx is not None
    emit_state = not has_s0
    u = _inproj(x2, p['norm_g'], scale, shift, p['in_w'].astype(BF16), rows_per_mod, BF16,
                *tiles['inproj'])

    bw = 256 if Wd % 256 == 0 else LANES
    nb = Wd // bw
    per = bw // bsz
    bd = p['qkv_bd'].reshape(3, nb, per, bsz, bsz)
    eye = jnp.eye(per, dtype=F32)
    wbd = jnp.einsum('tngio,gh->tngiho', bd, eye).reshape(3, nb, bw, bw).astype(BF16)
    n_gate = p['gate_w'].shape[-1]
    gw = p['gate_w'].reshape(2, 3, Wd, n_gate).transpose(1, 2, 0, 3).reshape(3, Wd, 2 * n_gate)
    gw = jnp.pad(gw, ((0, 0), (0, 0), (0, LANES - 2 * n_gate))).astype(BF16)

    tc = _tile(Wd, 512)
    nbt = tc // bw
    R = L * _tile(B, max(1, PRE_ROWS // L))
    xc, q, k, v, gates = pl.pallas_call(
        functools.partial(_mlpre_kernel, bw=bw, seq=L),
        grid=(rows // R, Wd // tc),
        in_specs=[pl.BlockSpec((R, tc), lambda b, j: (b, j)),
                  pl.BlockSpec((p['conv_w'].shape[0], tc), lambda b, j: (0, j)),
                  pl.BlockSpec((1, tc), lambda b, j: (0, j)),
                  pl.BlockSpec((3, nbt, bw, bw), lambda b, j: (0, j, 0, 0)),
                  pl.BlockSpec((3, tc, LANES), lambda b, j: (0, j, 0))],
        out_specs=[pl.BlockSpec((R, tc), lambda b, j: (b, j))] * 4
                  + [pl.BlockSpec((R, LANES), lambda b, j: (b, 0))],
        out_shape=[jax.ShapeDtypeStruct((rows, Wd), BF16)] * 4
                  + [jax.ShapeDtypeStruct((rows, LANES), F32)],
        compiler_params=_params("parallel", "arbitrary"),
        name="mlstm_pre",
    )(u, p['conv_w'], p['conv_b'].reshape(1, Wd), wbd, gw)

    gl = gates[:, :2 * n_gate].reshape(B, nc, CHUNK, 2, 2, H).transpose(0, 3, 5, 1, 4, 2)
    gb = p['gate_b'].reshape(2, 2, H).transpose(0, 2, 1).reshape(2, H, 2, 1)

    cps = _tile(nc, SCAN_CHUNKS_PER_STEP)
    nblk = nc // cps
    blk = cps * CHUNK

    def block_of(d, c):
        return c + d * (nblk - 1 - 2 * c)

    tok = lambda b, h, d, c: (b * nblk + block_of(d, c), h)
    in_specs = [pl.BlockSpec((1, 1, 1, cps, 2, CHUNK), lambda b, h, d, c: (b, d, h, block_of(d, c), 0, 0)),
                pl.BlockSpec((1, 1, 2, 1), lambda b, h, d, c: (d, h, 0, 0)),
                pl.BlockSpec((blk, dk), tok),
                pl.BlockSpec((blk, dk), tok),
                pl.BlockSpec((blk, dk), tok)]
    args = [gl, gb, q, k, v]
    c_spec = pl.BlockSpec((1, 1, 1, dk, dk), lambda b, h, d, c: (b, d, h, 0, 0))
    n_spec = pl.BlockSpec((1, 1, 1, 1, dk), lambda b, h, d, c: (b, d, h, 0, 0))
    m_spec = pl.BlockSpec((1, 1, 1, 1, 1), lambda b, h, d, c: (b, d, h, 0, 0))
    if has_s0:
        c0, n0, m0 = ctx
        in_specs += [c_spec, n_spec, m_spec]
        args += [c0, n0.reshape(B, 2, H, 1, dk), m0.reshape(B, 2, H, 1, 1)]
    out_specs = [pl.BlockSpec((1, blk, dk), lambda b, h, d, c: (d, b * nblk + block_of(d, c), h))]
    out_shape = [jax.ShapeDtypeStruct((2, rows, Wd), BF16)]
    if emit_state:
        out_specs += [c_spec, n_spec, m_spec]
        out_shape += [jax.ShapeDtypeStruct((B, 2, H, dk, dk), F32),
                      jax.ShapeDtypeStruct((B, 2, H, 1, dk), F32),
                      jax.ShapeDtypeStruct((B, 2, H, 1, 1), F32)]
    res = pl.pallas_call(
        functools.partial(_mlscan_kernel, nblk=nblk, cps=cps, dk=dk, has_s0=has_s0, emit_state=emit_state),
        grid=(B, H, 2, nblk),
        in_specs=in_specs,
        out_specs=out_specs,
        out_shape=out_shape,
        scratch_shapes=([] if emit_state else
                        [pltpu.VMEM((dk, dk), F32), pltpu.VMEM((1, dk), F32), pltpu.VMEM((1, 1), F32)]),
        compiler_params=_params("parallel", "parallel", "arbitrary", "arbitrary"),
        name="mlstm_scan",
    )(*args)
    hdir = res[0]

    tm = _tile(rows, 1024)
    o = pl.pallas_call(
        _mlpost_kernel,
        grid=(rows // tm, H),
        in_specs=[pl.BlockSpec((2, tm, dk), lambda i, h: (0, i, h)),
                  pl.BlockSpec((tm, dk), lambda i, h: (i, h)),
                  pl.BlockSpec((tm, dk), lambda i, h: (i, H + h)),
                  pl.BlockSpec((1, dk), lambda i, h: (0, h)),
                  pl.BlockSpec((1, dk), lambda i, h: (0, h))],
        out_specs=pl.BlockSpec((tm, dk), lambda i, h: (i, h)),
        out_shape=jax.ShapeDtypeStruct((rows, Wd), BF16),
        compiler_params=_params("parallel", "arbitrary"),
        name="mlstm_post",
    )(hdir, xc, u, p['gn_g'].reshape(1, Wd), p['skip'].reshape(1, Wd))
    y = _outproj(o, p['out_w'].astype(BF16), x2, gate, rows_per_mod, *tiles['outproj'])
    new = None
    if emit_state:
        new = (res[1], res[2].reshape(B, 2, H, dk), res[3].reshape(B, 2, H))
    return y, new


_BASE = dict(inproj=(1024, 2048), outproj=(512, 2048), tq=512)
_F32_OUT = dict(_BASE, inproj=(1024, 1024))
TILES_PROMPT = [_F32_OUT, _BASE, _BASE, _F32_OUT]
TILES_SAMPLE = [_BASE, _BASE, _BASE, _BASE]

def kernel(x_prompt, x_sample, c, c_ctx, cache_k_l0, cache_v_l0, state_ret_l1, state_C_l2, state_n_l2, state_m_l2, cache_k_l3, cache_v_l3, norm_g_l0, ada_w_l0, ada_b_l0, in_w_l0, out_w_l0, qk_g_l0, lam_l0, subln_g_l0, norm_g_l1, ada_w_l1, ada_b_l1, in_w_l1, out_w_l1, decay_l1, gn_g_l1, norm_g_l2, ada_w_l2, ada_b_l2, in_w_l2, out_w_l2, conv_w_l2, conv_b_l2, qkv_bd_l2, gate_w_l2, gate_b_l2, gn_g_l2, skip_l2, norm_g_l3, ada_w_l3, ada_b_l3, in_w_l3, out_w_l3, qk_g_l3, lam_l3, subln_g_l3):
    layers = [
        dict(norm_g=norm_g_l0, ada_w=ada_w_l0, ada_b=ada_b_l0, in_w=in_w_l0, out_w=out_w_l0,
             qk_g=qk_g_l0, lam=lam_l0, subln_g=subln_g_l0),
        dict(norm_g=norm_g_l1, ada_w=ada_w_l1, ada_b=ada_b_l1, in_w=in_w_l1, out_w=out_w_l1,
             decay=decay_l1, gn_g=gn_g_l1),
        dict(norm_g=norm_g_l2, ada_w=ada_w_l2, ada_b=ada_b_l2, in_w=in_w_l2, out_w=out_w_l2,
             conv_w=conv_w_l2, conv_b=conv_b_l2, qkv_bd=qkv_bd_l2, gate_w=gate_w_l2, gate_b=gate_b_l2,
             gn_g=gn_g_l2, skip=skip_l2),
        dict(norm_g=norm_g_l3, ada_w=ada_w_l3, ada_b=ada_b_l3, in_w=in_w_l3, out_w=out_w_l3,
             qk_g=qk_g_l3, lam=lam_l3, subln_g=subln_g_l3),
    ]
    ctxs = [(cache_k_l0, cache_v_l0), (state_ret_l1,), (state_C_l2, state_n_l2, state_m_l2),
            (cache_k_l3, cache_v_l3)]
    Bp, Lp, D = x_prompt.shape
    Bs, Ls, _ = x_sample.shape
    xp = x_prompt.reshape(Bp * Lp, D)
    xs = x_sample.reshape(Bs * Ls, D)
    n_cond = 1 + Bs
    cond_rows = -(-n_cond // 8) * 8
    cond = jnp.concatenate([c_ctx[None, :], c, jnp.zeros((cond_rows - n_cond, D), F32)], axis=0)
    new_states = []
    for i, p in enumerate(layers):
        m = _adaln(cond, p['ada_w'], p['ada_b'])
        mods_p = tuple(m[0:1, k * D:(k + 1) * D].reshape(1, 1, D) for k in range(3))
        mods_s = tuple(m[1:n_cond, k * D:(k + 1) * D].reshape(Bs, 1, D) for k in range(3))
        kind = i % 3
        tp, ts = TILES_PROMPT[i], TILES_SAMPLE[i]
        if kind == 0:
            xp, st = _diff_attention_layer(xp, Bp, Lp, p, mods_p, Bp * Lp, i, None, tp)
            xs, _ = _diff_attention_layer(xs, Bs, Ls, p, mods_s, Ls, i, ctxs[i], ts)
        elif kind == 1:
            xp, st = _retention_layer(xp, Bp, Lp, p, mods_p, Bp * Lp, None, tp)
            xs, _ = _retention_layer(xs, Bs, Ls, p, mods_s, Ls, ctxs[i], ts)
        else:
            xp, st = _mlstm_layer(xp, Bp, Lp, p, mods_p, Bp * Lp, None, tp)
            xs, _ = _mlstm_layer(xs, Bs, Ls, p, mods_s, Ls, ctxs[i], ts)
        new_states.append(st)
    (k_l0, v_l0), (ret_l1,), (C_l2, n_l2, m_l2), (k_l3, v_l3) = new_states
    return (xp.reshape(Bp, Lp, D), xs.reshape(Bs, Ls, D), k_l0, v_l0, ret_l1, C_l2, n_l2, m_l2, k_l3, v_l3)
```
